```python
import math
import jax
import jax.numpy as jnp
from jax import lax
import numpy as np

D_MODEL = 2048
BATCH = 2
SEQ = 4096
DEPTH = 4
DEC_BATCH = 8
DEC_SEQ = 8
PAST_LEN = 16384
PAGE_SIZE = 128

MIX_WIDTH = D_MODEL // 2
N_BRANCH = 3
ML_HEADS = 4
ML_DK = MIX_WIDTH // ML_HEADS
ML_DV = MIX_WIDTH // ML_HEADS
ML_QK = ML_HEADS * ML_DK
ML_CONV = 4
ML_CHUNK = 64
DA_HD = 64
DA_VD = 2 * DA_HD
DA_HEADS = MIX_WIDTH // DA_VD
DA_QK = DA_HEADS * 2 * DA_HD
ROPE_THETA = 10000.0
Q_BLOCK = 128
RW_HEAD = 64
RW_HEADS = MIX_WIDTH // RW_HEAD
RW_DECAY_LORA = max(32, int(round(1.8 * D_MODEL ** 0.5 / 32)) * 32)
RW_AAA_LORA = max(32, int(round(1.8 * D_MODEL ** 0.5 / 32)) * 32)
RW_GATE_LORA = max(32, int(round(0.6 * D_MODEL ** 0.8 / 32)) * 32)
RW_GN_EPS = 64e-5
RW_SPLITS = (MIX_WIDTH, MIX_WIDTH, MIX_WIDTH, RW_DECAY_LORA, RW_AAA_LORA, RW_GATE_LORA)
RW_SHIFT_WIDTH = sum(RW_SPLITS)
D_FF = ((8 * D_MODEL + 3 * 256 - 1) // (3 * 256)) * 256
IN_SPLITS = (ML_QK, ML_QK, MIX_WIDTH, MIX_WIDTH, ML_HEADS, ML_HEADS,
             DA_QK, DA_QK, MIX_WIDTH, RW_SHIFT_WIDTH, N_BRANCH * D_MODEL)
N_IN = sum(IN_SPLITS)
NORM_EPS = 1e-6

kernel_name = 'hybrid_mlstm_diffattn_rwkv7_step'


def split_cols(z, sizes):
    idx = np.cumsum(np.array(sizes))[:-1].tolist()
    return jnp.split(z, idx, axis=-1)


def rmsnorm(x, g):
    xf = x.astype(jnp.float32)
    y = xf * lax.rsqrt(jnp.mean(xf * xf, -1, keepdims=True) + NORM_EPS)
    return (y * g.astype(jnp.float32)).astype(x.dtype)


def headnorm(x, g, eps):
    xf = x.astype(jnp.float32)
    xc = xf - jnp.mean(xf, -1, keepdims=True)
    return xc * lax.rsqrt(jnp.mean(xc * xc, -1, keepdims=True) + eps) * g.astype(jnp.float32)


def rope(x, pos):
    half = DA_HD // 2
    inv = 1.0 / (ROPE_THETA ** (jnp.arange(0, DA_HD, 2, dtype=jnp.float32) / DA_HD))
    ang = pos.astype(jnp.float32)[:, None] * inv[None, :]
    cos = jnp.cos(ang)[None, :, None, None, :]
    sin = jnp.sin(ang)[None, :, None, None, :]
    xf = x.astype(jnp.float32)
    x1, x2 = xf[..., :half], xf[..., half:]
    return jnp.concatenate([x1 * cos - x2 * sin, x2 * cos + x1 * sin], -1).astype(x.dtype)


def causal_conv(u, buf, w):
    T = u.shape[1]
    full = jnp.concatenate([buf.astype(u.dtype), u], 1)
    y = full[:, 0:T] * w[0]
    for j in range(1, ML_CONV):
        y = y + full[:, j:j + T] * w[j]
    return y, full[:, T:]


def mlstm_chunkwise(q, k, v, li, lf, C0, n0, m0):
    B, T, H, _ = q.shape
    L = math.gcd(T, ML_CHUNK)
    nc = T // L

    def chunks(a):
        return jnp.moveaxis(a.astype(jnp.float32).reshape((B, nc, L) + a.shape[2:]), 1, 0)

    causal = jnp.tril(jnp.ones((L, L), dtype=bool))

    def step(carry, inp):
        C, n, m = carry
        qc, kc, vc, lic, lfc = inp
        b = jnp.cumsum(lfc, axis=1)
        bh = jnp.swapaxes(b, 1, 2)
        lih = jnp.swapaxes(lic, 1, 2)
        logD = jnp.where(causal, bh[..., :, None] - bh[..., None, :] + lih[..., None, :], -jnp.inf)
        log_inter = bh + m[..., None]
        m_t = jnp.maximum(log_inter, jnp.max(logD, -1))
        s = jnp.einsum('blhd,bshd->bhls', qc, kc) * jnp.exp(logD - m_t[..., None])
        inter = jnp.swapaxes(jnp.exp(log_inter - m_t), 1, 2)
        num = (jnp.einsum('bhls,bshe->blhe', s, vc)
               + jnp.einsum('blhd,bhde->blhe', qc, C) * inter[..., None])
        den = jnp.swapaxes(jnp.sum(s, -1), 1, 2) + jnp.einsum('blhd,bhd->blh', qc, n) * inter
        floor = jnp.exp(-jnp.swapaxes(m_t, 1, 2))
        h = num / jnp.maximum(jnp.abs(den), floor)[..., None]
        bT = b[:, -1]
        log_w = bT[:, None] - b + lic
        m_new = jnp.maximum(bT + m, jnp.max(log_w, 1))
        w = jnp.exp(log_w - m_new[:, None])
        dec = jnp.exp(bT + m - m_new)
        C = dec[..., None, None] * C + jnp.einsum('blh,blhd,blhe->bhde', w, kc, vc)
        n = dec[..., None] * n + jnp.einsum('blh,blhd->bhd', w, kc)
        return (C, n, m_new), h

    init = (C0.astype(jnp.float32), n0.astype(jnp.float32), m0.astype(jnp.float32))
    (C, n, m), h = lax.scan(step, init, (chunks(q), chunks(k), chunks(v), chunks(li), chunks(lf)))
    h = jnp.moveaxis(h, 0, 1).reshape(B, T, H, v.shape[-1])
    return h, C, n, m


def diff_attention(q, k, v, q_pos, k_pos, lam, lam_init, sub_g):
    B, Tq = q.shape[0], q.shape[1]
    scale = DA_HD ** -0.5
    kf = k.astype(jnp.float32)
    vf = v.astype(jnp.float32)

    def block(args):
        qb, pb = args
        s = jnp.einsum('bqhcd,bkhcd->bhcqk', qb.astype(jnp.float32), kf) * scale
        s = jnp.where(k_pos[None, :] <= pb[:, None], s, -jnp.inf)
        p = jax.nn.softmax(s, axis=-1)
        wgt = p[:, :, 0] - lam * p[:, :, 1]
        return jnp.einsum('bhqk,bkhe->bqhe', wgt, vf)

    qb_size = Q_BLOCK if Tq % Q_BLOCK == 0 else Tq
    nb = Tq // qb_size
    qs = jnp.moveaxis(q.reshape((B, nb, qb_size) + q.shape[2:]), 1, 0)
    ps = q_pos.reshape(nb, qb_size)
    o = lax.map(block, (qs, ps))
    o = jnp.moveaxis(o, 0, 1).reshape(B, Tq, DA_HEADS, DA_VD)
    o = rmsnorm(o, sub_g) * (1.0 - lam_init)
    return o.reshape(B, Tq, MIX_WIDTH)


def rwkv7_scan(r, w, k, v, kk, a, S0):
    def seq(t):
        return jnp.moveaxis(t.astype(jnp.float32), 1, 0)

    def step(S, inp):
        r_t, w_t, k_t, v_t, kk_t, a_t = inp
        sa = jnp.einsum('bhvk,bhk->bhv', S, -kk_t)
        S = (S * w_t[:, :, None, :] + sa[..., None] * (kk_t * a_t)[:, :, None, :]
             + v_t[..., None] * k_t[:, :, None, :])
        return S, jnp.einsum('bhvk,bhk->bhv', S, r_t)

    S, y = lax.scan(step, S0.astype(jnp.float32), (seq(r), seq(w), seq(k), seq(v), seq(kk), seq(a)))
    return S, jnp.moveaxis(y, 0, 1)


def hybrid_layer(x, l, p, k_past, v_past, conv_buf, C0, n0, m0, shift_buf, S0):
    B, T, _ = x.shape
    Tp = k_past.shape[1]
    q_pos = Tp + jnp.arange(T)
    k_pos = jnp.arange(Tp + T)
    f32 = jnp.float32

    h = rmsnorm(x, p['norm1_g'][l])
    z = h @ p['w_in'][l]
    qm, km, vm, om, ig, fg, qd, kd, vd, rw, gates = split_cols(z, IN_SPLITS)

    u, conv_new = causal_conv(jnp.concatenate([qm, km], -1), conv_buf, p['ml_conv_w'][l])
    u = jax.nn.silu(u)
    q_m = u[..., :ML_QK].reshape(B, T, ML_HEADS, ML_DK)
    k_m = u[..., ML_QK:].reshape(B, T, ML_HEADS, ML_DK) * (ML_DK ** -0.5)
    v_m = vm.reshape(B, T, ML_HEADS, ML_DV)
    li = ig.astype(f32) + p['ml_ib'][l].astype(f32)
    lf = jax.nn.log_sigmoid(fg.astype(f32) + p['ml_fb'][l].astype(f32))
    hm, C, n, m = mlstm_chunkwise(q_m, k_m, v_m, li, lf, C0, n0, m0)
    hm = headnorm(hm, p['ml_norm_g'][l].reshape(ML_HEADS, ML_DV), NORM_EPS)
    out_a = (jax.nn.sigmoid(om.astype(f32)) * hm.reshape(B, T, MIX_WIDTH)).astype(x.dtype)

    q_d = rope(qd.reshape(B, T, DA_HEADS, 2, DA_HD), q_pos)
    k_d = rope(kd.reshape(B, T, DA_HEADS, 2, DA_HD), q_pos)
    k_rows = k_d.reshape(B, T, DA_HEADS, 2 * DA_HD)
    v_rows = vd.reshape(B, T, DA_HEADS, DA_VD)
    k_all = jnp.concatenate([k_past, k_rows], 1).reshape(B, Tp + T, DA_HEADS, 2, DA_HD)
    v_all = jnp.concatenate([v_past, v_rows], 1)
    lam_init = 0.8 - 0.6 * math.exp(-0.3 * l)
    lp = p['da_lam'][l].astype(f32)
    lam = jnp.exp(jnp.sum(lp[0] * lp[1])) - jnp.exp(jnp.sum(lp[2] * lp[3])) + lam_init
    out_b = diff_attention(q_d, k_all, v_all, q_pos, k_pos, lam, lam_init,
                           p['da_subln_g'][l]).astype(x.dtype)

    prev = jnp.concatenate([shift_buf[:, None].astype(rw.dtype), rw[:, :-1]], 1)
    mixed = rw + (prev - rw) * p['rw_mu'][l]
    shift_new = rw[:, -1]
    r, kr, vr, wd, ad, gd = split_cols(mixed.astype(f32), RW_SPLITS)
    w_log = -jax.nn.softplus(-(p['rw_w0'][l] + jnp.tanh(wd) @ p['rw_w2'][l])) - 0.5
    decay = jnp.exp(-jnp.exp(w_log.astype(f32)))
    a = jax.nn.sigmoid(p['rw_a0'][l] + ad @ p['rw_a2'][l]).astype(f32)
    g = (jax.nn.sigmoid(gd) @ p['rw_g2'][l]).astype(f32)
    kk = (kr * p['rw_kk'][l]).astype(f32).reshape(B, T, RW_HEADS, RW_HEAD)
    kk = kk * lax.rsqrt(jnp.maximum(jnp.sum(kk * kk, -1, keepdims=True), 1e-24))
    kr = (kr * (1.0 + (a - 1.0) * p['rw_ka'][l])).astype(f32)
    hd = (B, T, RW_HEADS, RW_HEAD)
    r_h, w_h, k_h, v_h, a_h = r.reshape(hd), decay.reshape(hd), kr.reshape(hd), vr.reshape(hd), a.reshape(hd)
    S, y = rwkv7_scan(r_h, w_h, k_h, v_h, kk, a_h, S0)
    y = (headnorm(y, p['rw_gn_g'][l].reshape(RW_HEADS, RW_HEAD), RW_GN_EPS)
         + p['rw_gn_b'][l].reshape(RW_HEADS, RW_HEAD).astype(f32))
    y = y + jnp.sum(r_h * k_h * p['rw_rk'][l].astype(f32), -1, keepdims=True) * v_h
    out_c = (y.reshape(B, T, MIX_WIDTH) * g).astype(x.dtype)

    br = jnp.stack([out_a, out_b, out_c], 2)
    proj = jnp.einsum('btiw,iwd->btid', br, p['w_branch'][l])
    gate = jax.nn.sigmoid(gates.reshape(B, T, N_BRANCH, D_MODEL))
    x = x + (jnp.sum(gate * proj, 2) @ p['w_out'][l]).astype(x.dtype)

    h2 = rmsnorm(x, p['norm2_g'][l])
    gf, uf = jnp.split(h2 @ p['w_ffn_in'][l], 2, axis=-1)
    x = x + ((jax.nn.silu(gf) * uf) @ p['w_ffn_out'][l]).astype(x.dtype)
    return x, k_rows, v_rows, conv_new, C, n, m, shift_new, S


def setup_inputs(seed: int = 0) -> dict:
    key = jax.random.key(seed)
    ks = jax.random.split(key, 40)
    f32 = jnp.float32

    def nrm(i, shape, s):
        return jax.random.normal(ks[i], shape, f32) * s

    def gain(i, shape):
        return 1.0 + 0.02 * jax.random.normal(ks[i], shape, f32)

    n_pages = PAST_LEN // PAGE_SIZE
    n_used = DEC_BATCH * n_pages
    n_pool = n_used + max(1, n_used // 4)
    page_table = jax.random.permutation(ks[0], n_pool)[:n_used].reshape(DEC_BATCH, n_pages).astype(jnp.int32)
    return {
        'x_prompt': nrm(1, (BATCH, SEQ, D_MODEL), 1.0),
        'x_sample': nrm(2, (DEC_BATCH, DEC_SEQ, D_MODEL), 1.0),
        'cache_k': nrm(3, (DEPTH, n_pool, PAGE_SIZE, DA_HEADS, 2 * DA_HD), 1.0),
        'cache_v': nrm(4, (DEPTH, n_pool, PAGE_SIZE, DA_HEADS, DA_VD), 1.0),
        'page_table': page_table,
        'state_ml_conv': nrm(5, (DEPTH, DEC_BATCH, ML_CONV - 1, 2 * ML_QK), 1.0),
        'state_ml_C': nrm(6, (DEPTH, DEC_BATCH, ML_HEADS, ML_DK, ML_DV), 0.05),
        'state_ml_n': nrm(7, (DEPTH, DEC_BATCH, ML_HEADS, ML_DK), 0.1),
        'state_ml_m': nrm(8, (DEPTH, DEC_BATCH, ML_HEADS), 0.5),
        'state_rw_shift': nrm(9, (DEPTH, DEC_BATCH, RW_SHIFT_WIDTH), 1.0),
        'state_rw_S': nrm(10, (DEPTH, DEC_BATCH, RW_HEADS, RW_HEAD, RW_HEAD), 0.1),
        'norm1_g': gain(11, (DEPTH, D_MODEL)),
        'w_in': nrm(12, (DEPTH, D_MODEL, N_IN), D_MODEL ** -0.5),
        'ml_conv_w': nrm(13, (DEPTH, ML_CONV, 2 * ML_QK), ML_CONV ** -0.5),
        'ml_ib': nrm(14, (DEPTH, ML_HEADS), 0.1),
        'ml_fb': 3.0 + 3.0 * jax.random.uniform(ks[15], (DEPTH, ML_HEADS), f32),
        'ml_norm_g': gain(16, (DEPTH, MIX_WIDTH)),
        'da_lam': nrm(17, (DEPTH, 4, DA_HD), 0.1),
        'da_subln_g': gain(18, (DEPTH, DA_VD)),
        'rw_mu': jax.random.uniform(ks[19], (DEPTH, RW_SHIFT_WIDTH), f32),
        'rw_w0': jax.random.uniform(ks[20], (DEPTH, MIX_WIDTH), f32, -6.0, -1.0),
        'rw_w2': nrm(21, (DEPTH, RW_DECAY_LORA, MIX_WIDTH), 0.5 * RW_DECAY_LORA ** -0.5),
        'rw_a0': nrm(22, (DEPTH, MIX_WIDTH), 0.1),
        'rw_a2': nrm(23, (DEPTH, RW_AAA_LORA, MIX_WIDTH), RW_AAA_LORA ** -0.5),
        'rw_g2': nrm(24, (DEPTH, RW_GATE_LORA, MIX_WIDTH), RW_GATE_LORA ** -0.5),
        'rw_kk': 0.85 + 0.02 * jax.random.normal(ks[25], (DEPTH, MIX_WIDTH), f32),
        'rw_ka': gain(26, (DEPTH, MIX_WIDTH)),
        'rw_rk': nrm(27, (DEPTH, RW_HEADS, RW_HEAD), 0.1),
        'rw_gn_g': gain(28, (DEPTH, MIX_WIDTH)),
        'rw_gn_b': nrm(29, (DEPTH, MIX_WIDTH), 0.01),
        'w_branch': nrm(30, (DEPTH, N_BRANCH, MIX_WIDTH, D_MODEL), MIX_WIDTH ** -0.5),
        'w_out': nrm(31, (DEPTH, D_MODEL, D_MODEL), D_MODEL ** -0.5),
        'norm2_g': gain(32, (DEPTH, D_MODEL)),
        'w_ffn_in': nrm(33, (DEPTH, D_MODEL, 2 * D_FF), D_MODEL ** -0.5),
        'w_ffn_out': nrm(34, (DEPTH, D_FF, D_MODEL), D_FF ** -0.5),
        'norm_f_g': gain(35, (D_MODEL,)),
    }


def reference(x_prompt, x_sample, cache_k, cache_v, page_table, state_ml_conv, state_ml_C,
              state_ml_n, state_ml_m, state_rw_shift, state_rw_S, norm1_g, w_in, ml_conv_w,
              ml_ib, ml_fb, ml_norm_g, da_lam, da_subln_g, rw_mu, rw_w0, rw_w2, rw_a0, rw_a2,
              rw_g2, rw_kk, rw_ka, rw_rk, rw_gn_g, rw_gn_b, w_branch, w_out, norm2_g,
              w_ffn_in, w_ffn_out, norm_f_g):
    p = dict(norm1_g=norm1_g, w_in=w_in, ml_conv_w=ml_conv_w, ml_ib=ml_ib, ml_fb=ml_fb,
             ml_norm_g=ml_norm_g, da_lam=da_lam, da_subln_g=da_subln_g, rw_mu=rw_mu,
             rw_w0=rw_w0, rw_w2=rw_w2, rw_a0=rw_a0, rw_a2=rw_a2, rw_g2=rw_g2, rw_kk=rw_kk,
             rw_ka=rw_ka, rw_rk=rw_rk, rw_gn_g=rw_gn_g, rw_gn_b=rw_gn_b, w_branch=w_branch,
             w_out=w_out, norm2_g=norm2_g, w_ffn_in=w_ffn_in, w_ffn_out=w_ffn_out)
    f32 = jnp.float32
    bp = x_prompt.shape[0]
    bs = x_sample.shape[0]
    n_pages = PAST_LEN // PAGE_SIZE

    xp = x_prompt
    outs_p = []
    for l in range(DEPTH):
        res = hybrid_layer(
            xp, l, p,
            jnp.zeros((bp, 0, DA_HEADS, 2 * DA_HD), xp.dtype),
            jnp.zeros((bp, 0, DA_HEADS, DA_VD), xp.dtype),
            jnp.zeros((bp, ML_CONV - 1, 2 * ML_QK), xp.dtype),
            jnp.zeros((bp, ML_HEADS, ML_DK, ML_DV), f32),
            jnp.zeros((bp, ML_HEADS, ML_DK), f32),
            jnp.zeros((bp, ML_HEADS), f32),
            jnp.zeros((bp, RW_SHIFT_WIDTH), xp.dtype),
            jnp.zeros((bp, RW_HEADS, RW_HEAD, RW_HEAD), f32))
        xp = res[0]
        outs_p.append(res[1:])
    y_prompt = rmsnorm(xp, norm_f_g)

    xs = x_sample
    outs_s = []
    for l in range(DEPTH):
        k_past = cache_k[l, page_table].reshape(bs, n_pages * PAGE_SIZE, DA_HEADS, 2 * DA_HD)
        v_past = cache_v[l, page_table].reshape(bs, n_pages * PAGE_SIZE, DA_HEADS, DA_VD)
        res = hybrid_layer(xs, l, p, k_past, v_past, state_ml_conv[l], state_ml_C[l],
                           state_ml_n[l], state_ml_m[l], state_rw_shift[l], state_rw_S[l])
        xs = res[0]
        outs_s.append(res[1:])
    y_sample = rmsnorm(xs, norm_f_g)

    def stack(outs, i):
        return jnp.stack([o[i] for o in outs])

    k_rows_p, k_rows_s = stack(outs_p, 0), stack(outs_s, 0)
    v_rows_p, v_rows_s = stack(outs_p, 1), stack(outs_s, 1)
    ml_conv_p, ml_conv_s = stack(outs_p, 2), stack(outs_s, 2)
    ml_C_p, ml_C_s = stack(outs_p, 3), stack(outs_s, 3)
    ml_n_p, ml_n_s = stack(outs_p, 4), stack(outs_s, 4)
    ml_m_p, ml_m_s = stack(outs_p, 5), stack(outs_s, 5)
    rw_shift_p, rw_shift_s = stack(outs_p, 6), stack(outs_s, 6)
    rw_S_p, rw_S_s = stack(outs_p, 7), stack(outs_s, 7)
    return (y_prompt, y_sample, k_rows_p, v_rows_p, k_rows_s, v_rows_s, ml_conv_p, ml_conv_s,
            ml_C_p, ml_C_s, ml_n_p, ml_n_s, ml_m_p, ml_m_s, rw_shift_p, rw_shift_s, rw_S_p, rw_S_s)
```

```python
import functools
import math

import jax
import jax.numpy as jnp
from jax import lax
from jax.experimental import pallas as pl
from jax.experimental.pallas import tpu as pltpu

F32 = jnp.float32
BF16 = jnp.bfloat16
HIGHEST = lax.Precision.HIGHEST

D_MODEL = 2048
DEPTH = 4
PAGE_SIZE = 128
MIX = D_MODEL // 2
ML_HEADS = 4
ML_DK = MIX // ML_HEADS
ML_CONV = 4
DA_HD = 64
DA_VD = 2 * DA_HD
DA_HEADS = MIX // DA_VD
ROPE_THETA = 10000.0
RW_HEAD = 64
RW_HEADS = MIX // RW_HEAD
RW_PAIRS = RW_HEADS // 2
RW_LORA = 96
RW_LORA_PAD = 128
RW_GATE = 256
RW_W = 3 * MIX + 2 * RW_LORA + RW_GATE
RW_W_PAD = 3 * MIX + 2 * RW_LORA_PAD + RW_GATE
RW_GN_EPS = 64e-5
D_FF = 5632
NORM_EPS = 1e-6
NEG = -1e30

Z_ML = 0
Z_DA = 4 * MIX
Z_RW = Z_DA + 3 * MIX
Z_GATE = Z_RW + RW_W_PAD
Z_IF = Z_GATE + 3 * D_MODEL
Z_IF_W = 512
N_Z = Z_IF + Z_IF_W

RW_CHUNK = 64
VMEM_LIMIT = 52 * 1024 * 1024


def _cparams(*sem):
    return pltpu.CompilerParams(dimension_semantics=sem, vmem_limit_bytes=VMEM_LIMIT)


def _sigmoid(x):
    return 1.0 / (1.0 + jnp.exp(-x))


def _softplus(x):
    return jnp.maximum(x, 0.0) + jnp.log1p(jnp.exp(-jnp.abs(x)))


def _dot(a, b):
    return jnp.dot(a, b, preferred_element_type=F32)


def _dot_nt(a, b):
    return lax.dot_general(a, b, (((1,), (1,)), ((), ())), preferred_element_type=F32)


def _dot_tn(a, b, precision=None):
    return lax.dot_general(a, b, (((0,), (0,)), ((), ())), preferred_element_type=F32, precision=precision)


def _pad_rows(x, rows):
    if x.shape[0] == rows:
        return x
    return jnp.concatenate([x, jnp.zeros((rows - x.shape[0],) + x.shape[1:], x.dtype)], axis=0)


def _norm_mm_kernel(x_ref, g_ref, w_ref, o_ref, hn_ref):
    @pl.when(pl.program_id(1) == 0)
    def _():
        x = x_ref[...]
        ms = jnp.mean(x * x, axis=-1, keepdims=True)
        hn_ref[...] = (x * lax.rsqrt(ms + NORM_EPS) * g_ref[...]).astype(BF16)

    o_ref[...] = _dot(hn_ref[...], w_ref[...])


def norm_matmul(x, g, w_all, l, tm, tn):
    M, D = x.shape
    N = w_all.shape[-1]
    return pl.pallas_call(
        _norm_mm_kernel,
        grid=(M // tm, N // tn),
        in_specs=[pl.BlockSpec((tm, D), lambda i, j: (i, 0)),
                  pl.BlockSpec((1, D), lambda i, j: (0, 0)),
                  pl.BlockSpec((None, D, tn), lambda i, j: (l, 0, j))],
        out_specs=pl.BlockSpec((tm, tn), lambda i, j: (i, j)),
        out_shape=jax.ShapeDtypeStruct((M, N), F32),
        scratch_shapes=[pltpu.VMEM((tm, D), BF16)],
        compiler_params=_cparams("parallel", "arbitrary"),
    )(x, g, w_all)


def _ffn_in_kernel(x_ref, g_ref, wg_ref, wu_ref, o_ref, hn_ref):
    @pl.when(pl.program_id(1) == 0)
    def _():
        x = x_ref[...]
        ms = jnp.mean(x * x, axis=-1, keepdims=True)
        hn_ref[...] = (x * lax.rsqrt(ms + NORM_EPS) * g_ref[...]).astype(BF16)

    h = hn_ref[...]
    gf = _dot(h, wg_ref[...])
    uf = _dot(h, wu_ref[...])
    o_ref[...] = (gf * _sigmoid(gf) * uf).astype(BF16)


def ffn_in(x, g, w_all, l, tm, tn):
    M, D = x.shape
    nj = D_FF // tn
    return pl.pallas_call(
        _ffn_in_kernel,
        grid=(M // tm, nj),
        in_specs=[pl.BlockSpec((tm, D), lambda i, j: (i, 0)),
                  pl.BlockSpec((1, D), lambda i, j: (0, 0)),
                  pl.BlockSpec((None, D, tn), lambda i, j: (l, 0, j)),
                  pl.BlockSpec((None, D, tn), lambda i, j: (l, 0, j + nj))],
        out_specs=pl.BlockSpec((tm, tn), lambda i, j: (i, j)),
        out_shape=jax.ShapeDtypeStruct((M, D_FF), BF16),
        scratch_shapes=[pltpu.VMEM((tm, D), BF16)],
        compiler_params=_cparams("parallel", "arbitrary"),
    )(x, g, w_all, w_all)


def _mm_res_kernel(a_ref, w_ref, r_ref, o_ref):
    o_ref[...] = r_ref[...] + _dot(a_ref[...], w_ref[...])


def matmul_residual(a, w_all, l, res, tm, tn):
    M, K = a.shape
    N = w_all.shape[-1]
    return pl.pallas_call(
        _mm_res_kernel,
        grid=(M // tm, N // tn),
        in_specs=[pl.BlockSpec((tm, K), lambda i, j: (i, 0)),
                  pl.BlockSpec((None, K, tn), lambda i, j: (l, 0, j)),
                  pl.BlockSpec((tm, tn), lambda i, j: (i, j))],
        out_specs=pl.BlockSpec((tm, tn), lambda i, j: (i, j)),
        out_shape=jax.ShapeDtypeStruct((M, N), F32),
        compiler_params=_cparams("parallel", "parallel"),
    )(a, w_all, res)


def _norm_kernel(x_ref, g_ref, o_ref):
    x = x_ref[...]
    ms = jnp.mean(x * x, axis=-1, keepdims=True)
    o_ref[...] = x * lax.rsqrt(ms + NORM_EPS) * g_ref[...]


def final_norm(x, g, tm):
    M, D = x.shape
    return pl.pallas_call(
        _norm_kernel,
        grid=(M // tm,),
        in_specs=[pl.BlockSpec((tm, D), lambda i: (i, 0)), pl.BlockSpec((1, D), lambda i: (0, 0))],
        out_specs=pl.BlockSpec((tm, D), lambda i: (i, 0)),
        out_shape=jax.ShapeDtypeStruct((M, D), F32),
        compiler_params=_cparams("parallel"),
    )(x, g)


def _merge_kernel(a_ref, b_ref, c_ref, g0_ref, g1_ref, g2_ref, w_ref, o_ref):
    acc = _sigmoid(g0_ref[...]) * _dot(a_ref[...].astype(BF16), w_ref[0])
    acc = acc + _sigmoid(g1_ref[...]) * _dot(b_ref[...].astype(BF16), w_ref[1])
    acc = acc + _sigmoid(g2_ref[...]) * _dot(c_ref[...].astype(BF16), w_ref[2])
    o_ref[...] = acc.astype(BF16)


def merge_branches(oa, ob, oc, z, wb_all, l, tm, tn):
    M = oa.shape[0]
    nj = D_MODEL // tn
    gb = Z_GATE // tn
    br = pl.BlockSpec((tm, MIX), lambda i, j: (i, 0))

    def gate(k):
        return pl.BlockSpec((tm, tn), lambda i, j: (i, gb + k * nj + j))

    return pl.pallas_call(
        _merge_kernel,
        grid=(M // tm, nj),
        in_specs=[br, br, br, gate(0), gate(1), gate(2),
                  pl.BlockSpec((None, 3, MIX, tn), lambda i, j: (l, 0, 0, j))],
        out_specs=pl.BlockSpec((tm, tn), lambda i, j: (i, j)),
        out_shape=jax.ShapeDtypeStruct((M, D_MODEL), BF16),
        compiler_params=_cparams("parallel", "parallel"),
    )(oa, ob, oc, z, z, z, wb_all)


def _mlstm_kernel(qk_ref, halo_ref, cbuf_ref, v_ref, om_ref, gc_ref, gt_ref, cw_ref, bcol_ref, brow_ref,
                  ng_ref, C0_ref, n0_ref, m0_ref,
                  oa_ref, C_ref, n_ref, m_ref, C_s, n_s, m_s, *, t_valid, lp, nc):
    c = pl.program_id(1)
    rows = qk_ref.shape[0]

    @pl.when(c == 0)
    def _():
        C_s[...] = C0_ref[...]
        n_s[...] = n0_ref[...]
        m_s[...] = m0_ref[...]

    u = _pad_rows(qk_ref[...], lp)
    halo = jnp.where(c == 0, cbuf_ref[...], halo_ref[...])
    cw = cw_ref[...]
    row8 = lax.broadcasted_iota(jnp.int32, (8, 1), 0)
    y = None
    for k in range(ML_CONV - 1, 0, -1):
        ur = pltpu.roll(u, k, 0)
        hr = pltpu.roll(halo, k, 0)
        first = jnp.where(row8 < k, hr, ur[:8])
        sh = jnp.concatenate([first, ur[8:]], axis=0)
        term = sh * cw[ML_CONV - 1 - k:ML_CONV - k]
        y = term if y is None else y + term
    y = y + u * cw[ML_CONV - 1:ML_CONV]
    act = y * _sigmoid(y)
    q = act[:, :MIX]
    k_all = act[:, MIX:] * (ML_DK ** -0.5)
    v = _pad_rows(v_ref[...], lp)
    om = om_ref[...]

    rowi = lax.broadcasted_iota(jnp.int32, (lp, lp), 0)
    coli = lax.broadcasted_iota(jnp.int32, (lp, lp), 1)
    causal = rowi >= coli
    tril = causal.astype(F32)
    triu = (rowi <= coli).astype(F32)
    valid_c = lax.broadcasted_iota(jnp.int32, (lp, 1), 0) < t_valid
    valid_r = lax.broadcasted_iota(jnp.int32, (1, lp), 1) < t_valid
    pre_c = _pad_rows(gc_ref[...], lp) + brow_ref[...]
    li_c = jnp.where(valid_c, pre_c, NEG)
    lf_c = jnp.where(valid_c, -_softplus(-pre_c), 0.0)
    b_c = jnp.dot(tril, lf_c, precision=HIGHEST, preferred_element_type=F32)
    pre_r = gt_ref[...] + bcol_ref[...]
    li_r = jnp.where(valid_r, pre_r, NEG)
    lf_r = jnp.where(valid_r, -_softplus(-pre_r), 0.0)
    b_r = jnp.dot(lf_r, triu, precision=HIGHEST, preferred_element_type=F32)

    ng = ng_ref[...]
    for h in range(ML_HEADS):
        sl = slice(h * ML_DK, (h + 1) * ML_DK)
        bc = b_c[:, ML_HEADS + h:ML_HEADS + h + 1]
        br = b_r[ML_HEADS + h:ML_HEADS + h + 1, :]
        lir = li_r[h:h + 1, :]
        lic = li_c[:, h:h + 1]
        m_prev = m_s[h:h + 1, 0:1]
        qh = q[:, sl]
        kh = k_all[:, sl]
        vh = v[:, sl].astype(BF16)
        qb = qh.astype(BF16)
        log_d = jnp.where(causal, bc - br + lir, NEG)
        log_inter = bc + m_prev
        m_t = jnp.maximum(log_inter, jnp.max(log_d, axis=-1, keepdims=True))
        s = _dot_nt(qb, kh.astype(BF16)) * jnp.exp(log_d - m_t)
        inter = jnp.exp(log_inter - m_t)
        C_h = C_s[h]
        num = _dot(s.astype(BF16), vh) + _dot(qb, C_h.astype(BF16)) * inter
        den = (jnp.sum(s, axis=-1, keepdims=True)
               + jnp.sum(qh * n_s[h:h + 1, :], axis=-1, keepdims=True) * inter)
        hh = num / jnp.maximum(jnp.abs(den), jnp.exp(-m_t))
        b_last = bc[lp - 1:lp, :]
        log_w_r = b_last - br + lir
        m_new = jnp.maximum(b_last + m_prev, jnp.max(log_w_r, axis=-1, keepdims=True))
        w_c = jnp.exp(b_last - bc + lic - m_new)
        dec = jnp.exp(b_last + m_prev - m_new)
        kw = kh * w_c
        C_s[h] = dec * C_h + _dot_tn(kw.astype(BF16), vh)
        n_s[h:h + 1, :] = dec * n_s[h:h + 1, :] + jnp.sum(kw, axis=0, keepdims=True)
        m_s[h:h + 1, :] = jnp.broadcast_to(m_new, (1, 128))
        mu = jnp.mean(hh, axis=-1, keepdims=True)
        xc = hh - mu
        var = jnp.mean(xc * xc, axis=-1, keepdims=True)
        hn = xc * lax.rsqrt(var + NORM_EPS) * ng[:, sl]
        oa_ref[:, sl] = _sigmoid(om[:, sl]) * hn[:rows]

    @pl.when(c == nc - 1)
    def _():
        C_ref[...] = C_s[...]
        n_ref[...] = n_s[...]
        m_ref[...] = m_s[...]


def mlstm_branch(z, gt, B, T, lc, conv_buf8, conv_w, bias_col, bias_row, norm_g, C0, n0, m0):
    nc = T // lc
    lp = gt.shape[-1] // nc
    lc8 = lc // 8
    kern = functools.partial(_mlstm_kernel, t_valid=lc, lp=lp, nc=nc)
    st = lambda b, c: (b, 0, 0)
    out = pl.pallas_call(
        kern,
        grid=(B, nc),
        in_specs=[
            pl.BlockSpec((lc, 2 * MIX), lambda b, c: (b * nc + c, 0)),
            pl.BlockSpec((8, 2 * MIX), lambda b, c: (jnp.maximum((b * nc + c) * lc8 - 1, 0), 0)),
            pl.BlockSpec((None, 8, 2 * MIX), lambda b, c: (b, 0, 0)),
            pl.BlockSpec((lc, MIX), lambda b, c: (b * nc + c, 2)),
            pl.BlockSpec((lc, MIX), lambda b, c: (b * nc + c, 3)),
            pl.BlockSpec((lc, 128), lambda b, c: (b * nc + c, Z_IF // 128)),
            pl.BlockSpec((None, 8, lp), lambda b, c: (b, 0, c)),
            pl.BlockSpec((ML_CONV, 2 * MIX), lambda b, c: (0, 0)),
            pl.BlockSpec((8, 1), lambda b, c: (0, 0)),
            pl.BlockSpec((1, 128), lambda b, c: (0, 0)),
            pl.BlockSpec((1, MIX), lambda b, c: (0, 0)),
            pl.BlockSpec((None, ML_HEADS, ML_DK, ML_DK), lambda b, c: (b, 0, 0, 0)),
            pl.BlockSpec((None, ML_HEADS, ML_DK), st),
            pl.BlockSpec((None, ML_HEADS, 128), st),
        ],
        out_specs=[
            pl.BlockSpec((lc, MIX), lambda b, c: (b * nc + c, 0)),
            pl.BlockSpec((None, ML_HEADS, ML_DK, ML_DK), lambda b, c: (b, 0, 0, 0)),
            pl.BlockSpec((None, ML_HEADS, ML_DK), st),
            pl.BlockSpec((None, ML_HEADS, 128), st),
        ],
        out_shape=[
            jax.ShapeDtypeStruct((B * T, MIX), F32),
            jax.ShapeDtypeStruct((B, ML_HEADS, ML_DK, ML_DK), F32),
            jax.ShapeDtypeStruct((B, ML_HEADS, ML_DK), F32),
            jax.ShapeDtypeStruct((B, ML_HEADS, 128), F32),
        ],
        scratch_shapes=[pltpu.VMEM((ML_HEADS, ML_DK, ML_DK), F32),
                        pltpu.VMEM((ML_HEADS, ML_DK), F32),
                        pltpu.VMEM((ML_HEADS, 128), F32)],
        compiler_params=_cparams("parallel", "arbitrary"),
    )(z, z, conv_buf8, z, z, z, gt, conv_w, bias_col, bias_row, norm_g, C0, n0, m0)
    return out


def _rope_kernel(q_ref, k_ref, v_ref, cos_ref, sin_ref, qo_ref, ko_ref, kb_ref, vo_ref, vb_ref):
    cos = cos_ref[...]
    sin = sin_ref[...]
    lane = lax.broadcasted_iota(jnp.int32, (1, MIX), 1)
    lo = (lane % DA_HD) < (DA_HD // 2)

    def rot(x):
        partner = jnp.where(lo, pltpu.roll(x, MIX - DA_HD // 2, 1), pltpu.roll(x, DA_HD // 2, 1))
        return x * cos + partner * sin

    qo_ref[...] = (rot(q_ref[...]) * (DA_HD ** -0.5)).astype(qo_ref.dtype)
    kr = rot(k_ref[...])
    ko_ref[...] = kr
    kb_ref[...] = kr.astype(BF16)
    v = v_ref[...]
    vo_ref[...] = v
    vb_ref[...] = v.astype(BF16)


def rope_qkv(z, cos, sin, tm, q_dtype):
    M = z.shape[0]
    nt = cos.shape[0] // tm
    cb = Z_DA // MIX
    row = lambda i: (i, 0)
    tab = pl.BlockSpec((tm, MIX), lambda i: (i % nt, 0))
    blk = pl.BlockSpec((tm, MIX), row)
    return pl.pallas_call(
        _rope_kernel,
        grid=(M // tm,),
        in_specs=[pl.BlockSpec((tm, MIX), lambda i: (i, cb)),
                  pl.BlockSpec((tm, MIX), lambda i: (i, cb + 1)),
                  pl.BlockSpec((tm, MIX), lambda i: (i, cb + 2)), tab, tab],
        out_specs=[blk, blk, blk, blk, blk],
        out_shape=[jax.ShapeDtypeStruct((M, MIX), q_dtype), jax.ShapeDtypeStruct((M, MIX), F32),
                   jax.ShapeDtypeStruct((M, MIX), BF16), jax.ShapeDtypeStruct((M, MIX), F32),
                   jax.ShapeDtypeStruct((M, MIX), BF16)],
        compiler_params=_cparams("parallel"),
    )(z, z, z, cos, sin)


def _softmax_step(s, v, m_ref, l_ref, acc_ref, idx):
    m_prev = m_ref[idx]
    m_new = jnp.maximum(m_prev, jnp.max(s, axis=-1, keepdims=True))
    alpha = jnp.exp(m_prev - m_new)
    p = jnp.exp(s - m_new[:, :1])
    l_ref[idx] = alpha * l_ref[idx] + jnp.sum(p, axis=-1, keepdims=True)
    acc_ref[idx] = alpha * acc_ref[idx] + _dot(p.astype(BF16), v)
    m_ref[idx] = m_new


def _subln(o, g, scale):
    ms = jnp.mean(o * o, axis=-1, keepdims=True)
    return o * lax.rsqrt(ms + NORM_EPS) * g * scale


def _flash_kernel(lam_ref, q_ref, k_ref, v_ref, g_ref, o_ref, m_s, l_s, acc_s, *, tq, out_scale):
    i = pl.program_id(2)
    q = q_ref[...]
    lane = lax.broadcasted_iota(jnp.int32, (1, DA_VD), 1)
    zero = jnp.zeros_like(q)
    qs = (jnp.where(lane < DA_HD, q, zero), jnp.where(lane >= DA_HD, q, zero))
    m_s[...] = jnp.full(m_s.shape, NEG, F32)
    l_s[...] = jnp.zeros(l_s.shape, F32)
    acc_s[...] = jnp.zeros(acc_s.shape, F32)
    causal = (lax.broadcasted_iota(jnp.int32, (tq, tq), 0) >= lax.broadcasted_iota(jnp.int32, (tq, tq), 1))

    def step(j, masked):
        off = pl.multiple_of(j * tq, tq)
        k = k_ref[pl.ds(off, tq), :]
        v = v_ref[pl.ds(off, tq), :]
        for c in range(2):
            s = _dot_nt(qs[c], k)
            if masked:
                s = jnp.where(causal, s, NEG)
            _softmax_step(s, v, m_s, l_s, acc_s, c)

    def body(j, carry):
        step(j, False)
        return carry

    lax.fori_loop(0, i, body, 0)
    step(i, True)
    o = acc_s[0] / l_s[0] - lam_ref[0] * (acc_s[1] / l_s[1])
    o_ref[...] = _subln(o, g_ref[...], out_scale)


def flash_diff_attention(lam, qb, kb, vb, sub_g, B, T, tq, out_scale):
    nq = T // tq
    kern = functools.partial(_flash_kernel, tq=tq, out_scale=out_scale)
    kv = pl.BlockSpec((T, DA_VD), lambda b, h, i: (b, h))
    return pl.pallas_call(
        kern,
        grid=(B, DA_HEADS, nq),
        in_specs=[pl.BlockSpec(memory_space=pltpu.SMEM),
                  pl.BlockSpec((tq, DA_VD), lambda b, h, i: (b * nq + i, h)), kv, kv,
                  pl.BlockSpec((1, DA_VD), lambda b, h, i: (0, 0))],
        out_specs=pl.BlockSpec((tq, DA_VD), lambda b, h, i: (b * nq + i, h)),
        out_shape=jax.ShapeDtypeStruct((B * T, MIX), F32),
        scratch_shapes=[pltpu.VMEM((2, tq, DA_VD), F32), pltpu.VMEM((2, tq, DA_VD), F32),
                        pltpu.VMEM((2, tq, DA_VD), F32)],
        compiler_params=_cparams("parallel", "parallel", "arbitrary"),
    )(lam, qb, kb, vb, sub_g)


def _dec_attn_kernel(pt_ref, lam_ref, q_ref, kc_ref, vc_ref, kn_ref, vn_ref, g_ref, o_ref,
                     qbd_s, m_s, l_s, acc_s, *, n_pages, tnew, out_scale):
    j = pl.program_id(1)
    lane = lax.broadcasted_iota(jnp.int32, (1, DA_VD), 1)

    @pl.when(j == 0)
    def _():
        q = q_ref[...].astype(F32)
        for h in range(DA_HEADS):
            qh = q[:, h * DA_VD:(h + 1) * DA_VD]
            q1 = jnp.where(lane < DA_HD, qh, 0.0)
            q2 = jnp.where(lane >= DA_HD, qh, 0.0)
            qbd_s[h] = jnp.concatenate([q1, q2], axis=0).astype(BF16)
        m_s[...] = jnp.full(m_s.shape, NEG, F32)
        l_s[...] = jnp.zeros(l_s.shape, F32)
        acc_s[...] = jnp.zeros(acc_s.shape, F32)

    def process(kf, vf, mask):
        for h in range(DA_HEADS):
            sl = slice(h * DA_VD, (h + 1) * DA_VD)
            s = _dot_nt(qbd_s[h], kf[:, sl].astype(BF16))
            if mask is not None:
                s = jnp.where(mask, s, NEG)
            _softmax_step(s, vf[:, sl].astype(BF16), m_s, l_s, acc_s, h)

    @pl.when(j < n_pages)
    def _():
        process(kc_ref[...], vc_ref[...], None)

    @pl.when(j == n_pages)
    def _():
        kn = _pad_rows(kn_ref[...], PAGE_SIZE)
        vn = _pad_rows(vn_ref[...], PAGE_SIZE)
        qi = lax.broadcasted_iota(jnp.int32, (2 * tnew, PAGE_SIZE), 0) % tnew
        ti = lax.broadcasted_iota(jnp.int32, (2 * tnew, PAGE_SIZE), 1)
        process(kn, vn, ti <= qi)
        lam = lam_ref[0]
        g = g_ref[...]
        for h in range(DA_HEADS):
            a = acc_s[h] / l_s[h]
            o = a[:tnew] - lam * a[tnew:]
            o_ref[:, h * DA_VD:(h + 1) * DA_VD] = _subln(o, g, out_scale)


def paged_diff_attention(page_table, lam, qb, cache_k, cache_v, l, k_new, v_new, sub_g, B, tnew, out_scale):
    n_pages = page_table.shape[1]
    kern = functools.partial(_dec_attn_kernel, n_pages=n_pages, tnew=tnew, out_scale=out_scale)
    page = pl.BlockSpec((None, None, PAGE_SIZE, MIX),
                        lambda b, j, pt: (l, pt[b, jnp.minimum(j, n_pages - 1)], 0, 0))
    rows = pl.BlockSpec((tnew, MIX), lambda b, j, pt: (b, 0))
    grid_spec = pltpu.PrefetchScalarGridSpec(
        num_scalar_prefetch=1,
        grid=(B, n_pages + 1),
        in_specs=[pl.BlockSpec(memory_space=pltpu.SMEM), rows, page, page, rows, rows,
                  pl.BlockSpec((1, DA_VD), lambda b, j, pt: (0, 0))],
        out_specs=rows,
        scratch_shapes=[pltpu.VMEM((DA_HEADS, 2 * tnew, DA_VD), BF16),
                        pltpu.VMEM((DA_HEADS, 2 * tnew, DA_VD), F32),
                        pltpu.VMEM((DA_HEADS, 2 * tnew, DA_VD), F32),
                        pltpu.VMEM((DA_HEADS, 2 * tnew, DA_VD), F32)],
    )
    return pl.pallas_call(
        kern,
        grid_spec=grid_spec,
        out_shape=jax.ShapeDtypeStruct((B * tnew, MIX), F32),
        compiler_params=_cparams("parallel", "arbitrary"),
    )(page_table, lam, qb, cache_k, cache_v, k_new, v_new, sub_g)


def _seg_sum(x, ones_bd):
    hi = x.astype(BF16)
    lo = (x - hi.astype(F32)).astype(BF16)
    return _dot(hi, ones_bd) + _dot(lo, ones_bd)


def _rwkv_prep_kernel(x_ref, halo_ref, shift_ref, mu_ref, w0_ref, a0_ref, kkw_ref, ka_ref, w2_ref, a2_ref,
                      g2_ref, ones_ref, r_ref, lw_ref, k_ref, v_ref, av_ref, bv_ref, g_ref, *, nt):
    i = pl.program_id(0)
    x = x_ref[...]
    tm = x.shape[0]
    prev_row = jnp.where(i % nt == 0, shift_ref[...], halo_ref[7:8, :])
    row = lax.broadcasted_iota(jnp.int32, (tm, 1), 0)
    prev = jnp.where(row == 0, prev_row, pltpu.roll(x, 1, 0))
    mixed = x + (prev - x) * mu_ref[...]
    r = mixed[:, :MIX]
    kr = mixed[:, MIX:2 * MIX]
    vr = mixed[:, 2 * MIX:3 * MIX]
    o = 3 * MIX
    wd = mixed[:, o:o + RW_LORA_PAD]
    ad = mixed[:, o + RW_LORA_PAD:o + 2 * RW_LORA_PAD]
    gd = mixed[:, o + 2 * RW_LORA_PAD:]
    wl = w0_ref[...] + _dot(jnp.tanh(wd).astype(BF16), w2_ref[...])
    w_log = -_softplus(-wl) - 0.5
    a = _sigmoid(a0_ref[...] + _dot(ad.astype(BF16), a2_ref[...]))
    kk = kr * kkw_ref[...]
    ss = _seg_sum(kk * kk, ones_ref[...])
    kk = kk * lax.rsqrt(jnp.maximum(ss, 1e-24))
    r_ref[...] = r
    lw_ref[...] = -jnp.exp(w_log)
    k_ref[...] = kr * (1.0 + (a - 1.0) * ka_ref[...])
    v_ref[...] = vr
    av_ref[...] = -kk
    bv_ref[...] = kk * a
    g_ref[...] = _dot(_sigmoid(gd).astype(BF16), g2_ref[...])


def rwkv_prep(z, shift_pad, nt, tm, mu, w0, a0, kkw, ka, w2, a2, g2, ones_bd):
    M = z.shape[0]
    cb = Z_RW // RW_W_PAD
    tm8 = tm // 8
    vec = pl.BlockSpec((1, MIX), lambda i: (0, 0))
    out = pl.BlockSpec((tm, MIX), lambda i: (i, 0))
    kern = functools.partial(_rwkv_prep_kernel, nt=nt)
    return pl.pallas_call(
        kern,
        grid=(M // tm,),
        in_specs=[pl.BlockSpec((tm, RW_W_PAD), lambda i: (i, cb)),
                  pl.BlockSpec((8, RW_W_PAD), lambda i: (jnp.maximum(i * tm8 - 1, 0), cb)),
                  pl.BlockSpec((None, 1, RW_W_PAD), lambda i: (i // nt, 0, 0)),
                  pl.BlockSpec((1, RW_W_PAD), lambda i: (0, 0)),
                  vec, vec, vec, vec,
                  pl.BlockSpec((RW_LORA_PAD, MIX), lambda i: (0, 0)),
                  pl.BlockSpec((RW_LORA_PAD, MIX), lambda i: (0, 0)),
                  pl.BlockSpec((RW_GATE, MIX), lambda i: (0, 0)),
                  pl.BlockSpec((MIX, MIX), lambda i: (0, 0))],
        out_specs=[out] * 7,
        out_shape=[jax.ShapeDtypeStruct((M, MIX), F32)] * 7,
        compiler_params=_cparams("parallel"),
    )(z, z, shift_pad, mu, w0, a0, kkw, ka, w2, a2, g2, ones_bd)


def _rwkv_chunk_kernel(r_ref, lw_ref, k_ref, v_ref, av_ref, bv_ref,
                       ah_ref, x_ref, rt_ref, arb_ref, yv_ref, bh_ref, z_ref, gc_ref):
    L = RW_CHUNK
    lw = _pad_rows(lw_ref[...], L)
    rowi = lax.broadcasted_iota(jnp.int32, (L, L), 0)
    coli = lax.broadcasted_iota(jnp.int32, (L, L), 1)
    tril = (rowi >= coli).astype(F32)
    c = jnp.dot(tril, lw, precision=HIGHEST, preferred_element_type=F32)
    c_last = c[L - 1:L, :]
    e_neg = jnp.exp(-c)
    e_last = jnp.exp(c_last - c)
    av = _pad_rows(av_ref[...], L)
    bv = _pad_rows(bv_ref[...], L)
    kv = _pad_rows(k_ref[...], L)
    vv = _pad_rows(v_ref[...], L)
    at = av * jnp.exp(c - lw)
    rt = _pad_rows(r_ref[...], L) * jnp.exp(c)
    bt = bv * e_neg
    kt = kv * e_neg
    bh = bv * e_last
    kh = kv * e_last
    rt_ref[...] = rt.astype(BF16)
    bh_ref[...] = bh.astype(BF16)

    lane = lax.broadcasted_iota(jnp.int32, (1, 128), 1)
    m0 = lane < RW_HEAD
    r2 = lax.broadcasted_iota(jnp.int32, (128, 128), 0)
    c2 = lax.broadcasted_iota(jnp.int32, (128, 128), 1)
    same = (r2 // RW_HEAD) == (c2 // RW_HEAD)
    mask_sl = same & ((r2 % RW_HEAD) > (c2 % RW_HEAD))
    mask_l = same & ((r2 % RW_HEAD) >= (c2 % RW_HEAD))
    eye = (r2 == c2).astype(F32)
    ones = jnp.ones((L, 128), F32)

    def split(x):
        return jnp.concatenate([jnp.where(m0, x, 0.0), jnp.where(m0, 0.0, x)], axis=0)

    def fold(x):
        return x[:L] + x[L:]

    for p in range(RW_PAIRS):
        sl = slice(p * 128, (p + 1) * 128)
        a_st = split(at[:, sl]).astype(BF16)
        r_st = split(rt[:, sl]).astype(BF16)
        b2 = bt[:, sl].astype(BF16)
        k2 = kt[:, sl].astype(BF16)
        lhs = jnp.concatenate([a_st, r_st], axis=0)
        rhs = jnp.concatenate([b2, b2, k2, k2], axis=0)
        G = _dot_nt(lhs, rhs)
        n_bd = jnp.where(mask_sl, G[:128, :128], 0.0)
        a_ak = jnp.where(mask_sl, G[:128, 128:], 0.0)
        a_rb = jnp.where(mask_l, G[128:, :128], 0.0)
        a_rk = jnp.where(mask_l, G[128:, 128:], 0.0)
        P = eye + n_bd
        Mx = n_bd
        for _ in range(5):
            mb = Mx.astype(BF16)
            Mx = _dot(mb, mb)
            P = P + _dot(P.astype(BF16), Mx.astype(BF16))
        Tb = P.astype(BF16)
        v_st = split(vv[:, sl]).astype(BF16)
        ah_ref[:, sl] = fold(_dot(Tb, a_st)).astype(BF16)
        x_ref[:, sl] = fold(_dot(Tb, _dot(a_ak.astype(BF16), v_st).astype(BF16)))
        yv_ref[:, sl] = fold(_dot(a_rk.astype(BF16), v_st))
        arb_ref[:, sl] = fold(a_rb).astype(BF16)
        z_ref[:, sl] = jnp.where(same, _dot_tn(kh[:, sl].astype(BF16), vv[:, sl].astype(BF16)), 0.0)
        gc_ref[:, sl] = jnp.exp(_dot_tn(lw[:, sl], ones, precision=HIGHEST))


def rwkv_chunk(r, lw, k, v, av, bv, rows):
    M = r.shape[0]
    nck = M // rows
    L = RW_CHUNK
    inp = pl.BlockSpec((rows, MIX), lambda i: (i, 0))
    o64 = pl.BlockSpec((L, MIX), lambda i: (i, 0))
    o128 = pl.BlockSpec((128, MIX), lambda i: (i, 0))
    s64 = lambda dt: jax.ShapeDtypeStruct((nck * L, MIX), dt)
    s128 = jax.ShapeDtypeStruct((nck * 128, MIX), F32)
    return pl.pallas_call(
        _rwkv_chunk_kernel,
        grid=(nck,),
        in_specs=[inp] * 6,
        out_specs=[o64, o64, o64, o64, o64, o64, o128, o128],
        out_shape=[s64(BF16), s64(F32), s64(BF16), s64(BF16), s64(F32), s64(BF16), s128, s128],
        compiler_params=_cparams("parallel"),
    )(r, lw, k, v, av, bv)


def _rwkv_seq_kernel(ah_ref, x_ref, rt_ref, arb_ref, yv_ref, bh_ref, z_ref, gc_ref, h0_ref,
                     y_ref, hout_ref, h_s, *, nb):
    L = RW_CHUNK

    @pl.when(pl.program_id(0) == 0)
    def _():
        h_s[...] = h0_ref[...]

    lane = lax.broadcasted_iota(jnp.int32, (1, 128), 1)
    m0 = lane < RW_HEAD
    r2 = lax.broadcasted_iota(jnp.int32, (128, 128), 0)
    c2 = lax.broadcasted_iota(jnp.int32, (128, 128), 1)
    same = (r2 // RW_HEAD) == (c2 // RW_HEAD)
    for b in range(nb):
        for p in range(RW_PAIRS):
            sl = slice(p * 128, (p + 1) * 128)
            H = h_s[b, p]
            Hb = H.astype(BF16)
            U = _dot(ah_ref[b, :, sl], Hb) + x_ref[b, :, sl]
            u_st = jnp.concatenate([jnp.where(m0, U, 0.0), jnp.where(m0, 0.0, U)], axis=0).astype(BF16)
            y_ref[b, :, sl] = _dot(rt_ref[b, :, sl], Hb) + _dot(arb_ref[b, :, sl], u_st) + yv_ref[b, :, sl]
            upd = _dot_tn(bh_ref[b, :, sl], U.astype(BF16))
            h_s[b, p] = gc_ref[b, :, sl] * H + jnp.where(same, upd, 0.0) + z_ref[b, :, sl]

    hout_ref[...] = h_s[...]


def rwkv_seq(ah, x, rt, arb, yv, bh, zb, gc, h0, B, nc):
    L = RW_CHUNK
    t3 = lambda a, n: a.reshape(B, nc * n, MIX)
    b64 = pl.BlockSpec((B, L, MIX), lambda c: (0, c, 0))
    b128 = pl.BlockSpec((B, 128, MIX), lambda c: (0, c, 0))
    hs = pl.BlockSpec((B, RW_PAIRS, 128, 128), lambda c: (0, 0, 0, 0))
    kern = functools.partial(_rwkv_seq_kernel, nb=B)
    return pl.pallas_call(
        kern,
        grid=(nc,),
        in_specs=[b64, b64, b64, b64, b64, b64, b128, b128, hs],
        out_specs=[b64, hs],
        out_shape=[jax.ShapeDtypeStruct((B, nc * L, MIX), F32),
                   jax.ShapeDtypeStruct((B, RW_PAIRS, 128, 128), F32)],
        scratch_shapes=[pltpu.VMEM((B, RW_PAIRS, 128, 128), F32)],
        compiler_params=_cparams("arbitrary"),
    )(t3(ah, L), t3(x, L), t3(rt, L), t3(arb, L), t3(yv, L), t3(bh, L), t3(zb, 128), t3(gc, 128), h0)


def _rwkv_post_kernel(y_ref, r_ref, k_ref, v_ref, g_ref, gn_g_ref, gn_b_ref, rk_ref, ones_ref, o_ref):
    ones_bd = ones_ref[...]
    y = y_ref[...]
    inv = 1.0 / RW_HEAD
    mu = _seg_sum(y, ones_bd) * inv
    xc = y - mu
    var = _seg_sum(xc * xc, ones_bd) * inv
    yn = xc * lax.rsqrt(var + RW_GN_EPS) * gn_g_ref[...] + gn_b_ref[...]
    bonus = _seg_sum(r_ref[...] * k_ref[...] * rk_ref[...], ones_bd)
    o_ref[...] = (yn + bonus * v_ref[...]) * g_ref[...]


def rwkv_post(y3, r, k, v, g, gn_g, gn_b, rk, ones_bd, B, T, tm):
    nt = T // tm
    t3 = lambda a: a.reshape(B, T, MIX)
    blk = pl.BlockSpec((None, tm, MIX), lambda b, i: (b, i, 0))
    vec = pl.BlockSpec((1, MIX), lambda b, i: (0, 0))
    out = pl.pallas_call(
        _rwkv_post_kernel,
        grid=(B, nt),
        in_specs=[blk, blk, blk, blk, blk, vec, vec, vec, pl.BlockSpec((MIX, MIX), lambda b, i: (0, 0))],
        out_specs=blk,
        out_shape=jax.ShapeDtypeStruct((B, T, MIX), F32),
        compiler_params=_cparams("parallel", "parallel"),
    )(y3, t3(r), t3(k), t3(v), t3(g), gn_g, gn_b, rk, ones_bd)
    return out.reshape(B * T, MIX)


def _regroup_w_in(w_in):
    o = 0

    def take(n):
        nonlocal o
        s = w_in[:, :, o:o + n]
        o += n
        return s

    qm, km, vm, om = take(MIX), take(MIX), take(MIX), take(MIX)
    ig, fg = take(ML_HEADS), take(ML_HEADS)
    qd, kd, vd = take(MIX), take(MIX), take(MIX)
    rw = take(RW_W)
    gates = take(3 * D_MODEL)
    zpad = lambda n: jnp.zeros(w_in.shape[:2] + (n,), w_in.dtype)
    rw_main = rw[:, :, :3 * MIX]
    wd = rw[:, :, 3 * MIX:3 * MIX + RW_LORA]
    ad = rw[:, :, 3 * MIX + RW_LORA:3 * MIX + 2 * RW_LORA]
    gd = rw[:, :, 3 * MIX + 2 * RW_LORA:]
    lp = RW_LORA_PAD - RW_LORA
    cols = [qm, km, vm, om, qd, kd, vd, rw_main, wd, zpad(lp), ad, zpad(lp), gd, gates,
            ig, fg, zpad(Z_IF_W - 2 * ML_HEADS)]
    return jnp.concatenate(cols, axis=-1).astype(BF16)


def _pad_rw_row(v):
    lp = RW_LORA_PAD - RW_LORA
    z = jnp.zeros(v.shape[:-1] + (lp,), v.dtype)
    o = 3 * MIX
    return jnp.concatenate([v[..., :o], v[..., o:o + RW_LORA], z, v[..., o + RW_LORA:o + 2 * RW_LORA], z,
                            v[..., o + 2 * RW_LORA:]], axis=-1)


def _unpad_rw_row(v):
    o = 3 * MIX
    return jnp.concatenate([v[..., :o], v[..., o:o + RW_LORA], v[..., o + RW_LORA_PAD:o + RW_LORA_PAD + RW_LORA],
                            v[..., o + 2 * RW_LORA_PAD:]], axis=-1)


def _rope_tables(pos):
    half = DA_HD // 2
    inv = 1.0 / (ROPE_THETA ** (jnp.arange(0, DA_HD, 2, dtype=F32) / DA_HD))
    ang = pos.astype(F32)[:, None] * inv[None, :]
    cos = jnp.cos(ang)
    sin = jnp.sin(ang)
    cos_g = jnp.concatenate([cos, cos], axis=-1)
    sin_g = jnp.concatenate([-sin, sin], axis=-1)
    reps = MIX // DA_HD
    return jnp.tile(cos_g, (1, reps)), jnp.tile(sin_g, (1, reps))


def _pack_rw_state(S):
    B = S.shape[0]
    St = jnp.swapaxes(S, -1, -2).reshape(B, RW_PAIRS, 2, RW_HEAD, RW_HEAD)
    z = jnp.zeros_like(St[:, :, 0])
    top = jnp.concatenate([St[:, :, 0], z], axis=-1)
    bot = jnp.concatenate([z, St[:, :, 1]], axis=-1)
    return jnp.concatenate([top, bot], axis=-2)


def _unpack_rw_state(Hbd):
    B = Hbd.shape[0]
    h0 = Hbd[:, :, :RW_HEAD, :RW_HEAD]
    h1 = Hbd[:, :, RW_HEAD:, RW_HEAD:]
    St = jnp.stack([h0, h1], axis=2).reshape(B, RW_HEADS, RW_HEAD, RW_HEAD)
    return jnp.swapaxes(St, -1, -2)


def _layer(x, l, W, P, grp):
    B, T = grp["B"], grp["T"]
    M = B * T
    tm = grp["tm"]
    z = norm_matmul(x, P["norm1_g"][l][None], W["w_in"], l, tm, 512)

    lc = grp["ml_chunk"]
    lp = grp["ml_pad"]
    g8 = z[:, Z_IF:Z_IF + 8].reshape(B, T, 8)
    gt = jnp.swapaxes(g8, 1, 2)
    if lp != lc:
        gt = jnp.pad(gt, ((0, 0), (0, 0), (0, lp - lc)))
    bias = jnp.concatenate([P["ml_ib"][l], P["ml_fb"][l]])
    bias_row = jnp.pad(bias, (0, 120))[None]
    oa, C, n, m = mlstm_branch(z, gt, B, T, lc, grp["conv8"][l], P["ml_conv_w"][l], bias[:, None], bias_row,
                               P["ml_norm_g"][l][None], grp["C0"][l], grp["n0"][l], grp["m0"][l])
    conv_new = z[:, :2 * MIX].reshape(B, T, 2 * MIX)[:, T - (ML_CONV - 1):]

    qb, k_rot, kb, v_rows, vb = rope_qkv(z, grp["cos"], grp["sin"], grp["rope_tm"],
                                         F32 if grp["paged"] else BF16)
    lam_init = 0.8 - 0.6 * math.exp(-0.3 * l)
    lpar = P["da_lam"][l]
    lam = (jnp.exp(jnp.sum(lpar[0] * lpar[1])) - jnp.exp(jnp.sum(lpar[2] * lpar[3])) + lam_init).reshape(1)
    sub_g = P["da_subln_g"][l][None]
    if grp["paged"]:
        ob = paged_diff_attention(grp["page_table"], lam, qb, grp["cache_k"], grp["cache_v"], l, k_rot, v_rows,
                                  sub_g, B, T, 1.0 - lam_init)
    else:
        ob = flash_diff_attention(lam, qb, kb, vb, sub_g, B, T, grp["tq"], 1.0 - lam_init)

    rtm = grp["rw_tm"]
    r, lw, kmod, vr, av, bv, g = rwkv_prep(z, grp["shift"][l], T // rtm, rtm, W["rw_mu"][l], P["rw_w0"][l][None],
                                           P["rw_a0"][l][None], P["rw_kk"][l][None], P["rw_ka"][l][None],
                                           W["rw_w2"][l], W["rw_a2"][l], W["rw_g2"][l], W["ones_bd"])
    rows = grp["rw_rows"]
    nc = T // rows
    pre = rwkv_chunk(r, lw, kmod, vr, av, bv, rows)
    y3, Hbd = rwkv_seq(*pre, grp["H0"][l], B, nc)
    oc = rwkv_post(y3, r, kmod, vr, g, P["rw_gn_g"][l][None], P["rw_gn_b"][l][None],
                   P["rw_rk"][l].reshape(1, MIX), W["ones_bd"], B, T, grp["post_tm"])
    shift_new = _unpad_rw_row(z[:, Z_RW:Z_RW + RW_W_PAD].reshape(B, T, RW_W_PAD)[:, -1])
    S = _unpack_rw_state(Hbd)

    merged = merge_branches(oa, ob, oc, z, W["w_branch"], l, tm, 512)
    x = matmul_residual(merged, W["w_out"], l, x, tm, 512)
    act = ffn_in(x, P["norm2_g"][l][None], W["w_ffn_in"], l, tm, 512)
    x = matmul_residual(act, W["w_ffn_out"], l, x, grp["ffn_out_tm"], 512)
    outs = (k_rot.reshape(B, T, DA_HEADS, DA_VD), v_rows.reshape(B, T, DA_HEADS, DA_VD), conv_new,
            C, n, m[:, :, 0], shift_new, S)
    return x, outs


def _prep_weights(w_in, w_branch, w_out, w_ffn_in, w_ffn_out, rw_mu, rw_w2, rw_a2, rw_g2):
    lpad = RW_LORA_PAD - RW_LORA
    head = jnp.arange(MIX) // RW_HEAD
    return dict(
        w_in=_regroup_w_in(w_in),
        w_branch=w_branch.astype(BF16),
        w_out=w_out.astype(BF16),
        w_ffn_in=w_ffn_in.astype(BF16),
        w_ffn_out=w_ffn_out.astype(BF16),
        rw_mu=_pad_rw_row(rw_mu)[:, None],
        rw_w2=jnp.pad(rw_w2, ((0, 0), (0, lpad), (0, 0))).astype(BF16),
        rw_a2=jnp.pad(rw_a2, ((0, 0), (0, lpad), (0, 0))).astype(BF16),
        rw_g2=rw_g2.astype(BF16),
        ones_bd=(head[:, None] == head[None, :]).astype(BF16),
    )


def _prompt_group(bp, T, depth):
    zeros = lambda *s: jnp.zeros(s, F32)
    cos, sin = _rope_tables(jnp.arange(T))
    M = bp * T
    return dict(
        B=bp, T=T, tm=min(1024, M), ffn_out_tm=min(512, M), ml_chunk=min(256, T), ml_pad=min(256, T),
        rope_tm=min(512, T), tq=min(512, T), paged=False, rw_tm=min(256, T), rw_rows=RW_CHUNK,
        post_tm=min(512, T), norm_tm=min(512, M), cos=cos, sin=sin,
        conv8=zeros(depth, bp, 8, 2 * MIX), C0=zeros(depth, bp, ML_HEADS, ML_DK, ML_DK),
        n0=zeros(depth, bp, ML_HEADS, ML_DK), m0=zeros(depth, bp, ML_HEADS, 128),
        shift=zeros(depth, bp, 1, RW_W_PAD), H0=zeros(depth, bp, RW_PAIRS, 128, 128),
    )


def _sample_group(bs, ts, depth, cache_k, cache_v, page_table, state_ml_conv, state_ml_C, state_ml_n, state_ml_m,
                  state_rw_shift, state_rw_S):
    past = page_table.shape[1] * PAGE_SIZE
    cos, sin = _rope_tables(jnp.tile(past + jnp.arange(ts), bs))
    M = bs * ts
    return dict(
        B=bs, T=ts, tm=M, ffn_out_tm=M, ml_chunk=ts, ml_pad=RW_CHUNK, rope_tm=M, paged=True,
        rw_tm=ts, rw_rows=ts, post_tm=ts, norm_tm=M, cos=cos, sin=sin,
        page_table=page_table,
        cache_k=cache_k.reshape(cache_k.shape[0], cache_k.shape[1], PAGE_SIZE, MIX),
        cache_v=cache_v.reshape(cache_v.shape[0], cache_v.shape[1], PAGE_SIZE, MIX),
        conv8=jnp.pad(state_ml_conv, ((0, 0), (0, 0), (8 - (ML_CONV - 1), 0), (0, 0))),
        C0=state_ml_C, n0=state_ml_n, m0=jnp.broadcast_to(state_ml_m[..., None], state_ml_m.shape + (128,)),
        shift=_pad_rw_row(state_rw_shift)[:, :, None],
        H0=jnp.stack([_pack_rw_state(state_rw_S[l]) for l in range(depth)]),
    )


def kernel(x_prompt, x_sample, cache_k, cache_v, page_table, state_ml_conv, state_ml_C, state_ml_n, state_ml_m,
           state_rw_shift, state_rw_S, norm1_g, w_in, ml_conv_w, ml_ib, ml_fb, ml_norm_g, da_lam, da_subln_g,
           rw_mu, rw_w0, rw_w2, rw_a0, rw_a2, rw_g2, rw_kk, rw_ka, rw_rk, rw_gn_g, rw_gn_b, w_branch, w_out,
           norm2_g, w_ffn_in, w_ffn_out, norm_f_g):
    P = dict(norm1_g=norm1_g, ml_conv_w=ml_conv_w, ml_ib=ml_ib, ml_fb=ml_fb, ml_norm_g=ml_norm_g, da_lam=da_lam,
             da_subln_g=da_subln_g, rw_w0=rw_w0, rw_a0=rw_a0, rw_kk=rw_kk, rw_ka=rw_ka, rw_rk=rw_rk,
             rw_gn_g=rw_gn_g, rw_gn_b=rw_gn_b, norm2_g=norm2_g)
    depth = w_in.shape[0]
    W = _prep_weights(w_in, w_branch, w_out, w_ffn_in, w_ffn_out, rw_mu, rw_w2, rw_a2, rw_g2)
    bp, T, _ = x_prompt.shape
    bs, ts, _ = x_sample.shape
    grp_p = _prompt_group(bp, T, depth)
    grp_s = _sample_group(bs, ts, depth, cache_k, cache_v, page_table, state_ml_conv, state_ml_C, state_ml_n,
                          state_ml_m, state_rw_shift, state_rw_S)

    results = []
    for x0, grp in ((x_prompt, grp_p), (x_sample, grp_s)):
        x = x0.reshape(grp["B"] * grp["T"], D_MODEL)
        outs = []
        for l in range(depth):
            x, o = _layer(x, l, W, P, grp)
            outs.append(o)
        y = final_norm(x, norm_f_g[None], grp["norm_tm"]).reshape(x0.shape)
        results.append((y, [jnp.stack([o[i] for o in outs]) for i in range(8)]))

    (y_p, sp), (y_s, ss) = results
    out = [y_p, y_s, sp[0], sp[1], ss[0], ss[1]]
    for i in range(2, 8):
        out += [sp[i], ss[i]]
    return tuple(out)
```

```python
import functools
import math

import jax
import jax.numpy as jnp
from jax import lax
from jax.experimental import pallas as pl
from jax.experimental.pallas import tpu as pltpu

F32 = jnp.float32
BF16 = jnp.bfloat16
HIGHEST = lax.Precision.HIGHEST

D_MODEL = 2048
DEPTH = 4
PAGE_SIZE = 128
MIX = D_MODEL // 2
ML_HEADS = 4
ML_DK = MIX // ML_HEADS
ML_CONV = 4
DA_HD = 64
DA_VD = 2 * DA_HD
DA_HEADS = MIX // DA_VD
ROPE_THETA = 10000.0
RW_HEAD = 64
RW_HEADS = MIX // RW_HEAD
RW_PAIRS = RW_HEADS // 2
RW_LORA = 96
RW_LORA_PAD = 128
RW_GATE = 256
RW_W = 3 * MIX + 2 * RW_LORA + RW_GATE
RW_W_PAD = 3 * MIX + 2 * RW_LORA_PAD + RW_GATE
RW_GN_EPS = 64e-5
D_FF = 5632
NORM_EPS = 1e-6
NEG = -1e30

Z_ML = 0
Z_DA = 4 * MIX
Z_RW = Z_DA + 3 * MIX
Z_GATE = Z_RW + RW_W_PAD
Z_IF = Z_GATE + 3 * D_MODEL
Z_IF_W = 512
N_Z = Z_IF + Z_IF_W

RW_CHUNK = 64
VMEM_LIMIT = 52 * 1024 * 1024


def _cparams(*sem):
    return pltpu.CompilerParams(dimension_semantics=sem, vmem_limit_bytes=VMEM_LIMIT)


def _sigmoid(x):
    return 1.0 / (1.0 + jnp.exp(-x))


def _softplus(x):
    return jnp.maximum(x, 0.0) + jnp.log1p(jnp.exp(-jnp.abs(x)))


def _dot(a, b):
    return jnp.dot(a, b, preferred_element_type=F32)


def _dot_nt(a, b):
    return lax.dot_general(a, b, (((1,), (1,)), ((), ())), preferred_element_type=F32)


def _dot_tn(a, b, precision=None):
    return lax.dot_general(a, b, (((0,), (0,)), ((), ())), preferred_element_type=F32, precision=precision)


def _pad_rows(x, rows):
    if x.shape[0] == rows:
        return x
    return jnp.concatenate([x, jnp.zeros((rows - x.shape[0],) + x.shape[1:], x.dtype)], axis=0)


def _norm_mm_kernel(x_ref, g_ref, w_ref, o_ref, hn_ref):
    @pl.when(pl.program_id(1) == 0)
    def _():
        x = x_ref[...]
        ms = jnp.mean(x * x, axis=-1, keepdims=True)
        hn_ref[...] = (x * lax.rsqrt(ms + NORM_EPS) * g_ref[...]).astype(BF16)

    o_ref[...] = _dot(hn_ref[...], w_ref[...])


def norm_matmul(x, g, w_all, l, tm, tn):
    M, D = x.shape
    N = w_all.shape[-1]
    return pl.pallas_call(
        _norm_mm_kernel,
        grid=(M // tm, N // tn),
        in_specs=[pl.BlockSpec((tm, D), lambda i, j: (i, 0)),
                  pl.BlockSpec((1, D), lambda i, j: (0, 0)),
                  pl.BlockSpec((None, D, tn), lambda i, j: (l, 0, j))],
        out_specs=pl.BlockSpec((tm, tn), lambda i, j: (i, j)),
        out_shape=jax.ShapeDtypeStruct((M, N), F32),
        scratch_shapes=[pltpu.VMEM((tm, D), BF16)],
        compiler_params=_cparams("parallel", "arbitrary"),
        name="norm_matmul",
    )(x, g, w_all)


def _ffn_in_kernel(x_ref, g_ref, wg_ref, wu_ref, o_ref, hn_ref):
    @pl.when(pl.program_id(1) == 0)
    def _():
        x = x_ref[...]
        ms = jnp.mean(x * x, axis=-1, keepdims=True)
        hn_ref[...] = (x * lax.rsqrt(ms + NORM_EPS) * g_ref[...]).astype(BF16)

    h = hn_ref[...]
    gf = _dot(h, wg_ref[...])
    uf = _dot(h, wu_ref[...])
    o_ref[...] = (gf * _sigmoid(gf) * uf).astype(BF16)


def ffn_in(x, g, w_all, l, tm, tn):
    M, D = x.shape
    nj = D_FF // tn
    return pl.pallas_call(
        _ffn_in_kernel,
        grid=(M // tm, nj),
        in_specs=[pl.BlockSpec((tm, D), lambda i, j: (i, 0)),
                  pl.BlockSpec((1, D), lambda i, j: (0, 0)),
                  pl.BlockSpec((None, D, tn), lambda i, j: (l, 0, j)),
                  pl.BlockSpec((None, D, tn), lambda i, j: (l, 0, j + nj))],
        out_specs=pl.BlockSpec((tm, tn), lambda i, j: (i, j)),
        out_shape=jax.ShapeDtypeStruct((M, D_FF), BF16),
        scratch_shapes=[pltpu.VMEM((tm, D), BF16)],
        compiler_params=_cparams("parallel", "arbitrary"),
        name="ffn_in",
    )(x, g, w_all, w_all)


def _mm_res_kernel(a_ref, w_ref, r_ref, o_ref):
    o_ref[...] = r_ref[...] + _dot(a_ref[...], w_ref[...])


def matmul_residual(a, w_all, l, res, tm, tn):
    M, K = a.shape
    N = w_all.shape[-1]
    return pl.pallas_call(
        _mm_res_kernel,
        grid=(M // tm, N // tn),
        in_specs=[pl.BlockSpec((tm, K), lambda i, j: (i, 0)),
                  pl.BlockSpec((None, K, tn), lambda i, j: (l, 0, j)),
                  pl.BlockSpec((tm, tn), lambda i, j: (i, j))],
        out_specs=pl.BlockSpec((tm, tn), lambda i, j: (i, j)),
        out_shape=jax.ShapeDtypeStruct((M, N), F32),
        compiler_params=_cparams("parallel", "parallel"),
        name="matmul_residual",
    )(a, w_all, res)


def _norm_kernel(x_ref, g_ref, o_ref):
    x = x_ref[...]
    ms = jnp.mean(x * x, axis=-1, keepdims=True)
    o_ref[...] = x * lax.rsqrt(ms + NORM_EPS) * g_ref[...]


def final_norm(x, g, tm):
    M, D = x.shape
    return pl.pallas_call(
        _norm_kernel,
        grid=(M // tm,),
        in_specs=[pl.BlockSpec((tm, D), lambda i: (i, 0)), pl.BlockSpec((1, D), lambda i: (0, 0))],
        out_specs=pl.BlockSpec((tm, D), lambda i: (i, 0)),
        out_shape=jax.ShapeDtypeStruct((M, D), F32),
        compiler_params=_cparams("parallel"),
        name="final_norm",
    )(x, g)


def _merge_kernel(a_ref, b_ref, c_ref, g0_ref, g1_ref, g2_ref, w_ref, o_ref):
    acc = _sigmoid(g0_ref[...]) * _dot(a_ref[...].astype(BF16), w_ref[0])
    acc = acc + _sigmoid(g1_ref[...]) * _dot(b_ref[...].astype(BF16), w_ref[1])
    acc = acc + _sigmoid(g2_ref[...]) * _dot(c_ref[...].astype(BF16), w_ref[2])
    o_ref[...] = acc.astype(BF16)


def merge_branches(oa, ob, oc, z, wb_all, l, tm, tn):
    M = oa.shape[0]
    nj = D_MODEL // tn
    gb = Z_GATE // tn
    br = pl.BlockSpec((tm, MIX), lambda i, j: (i, 0))

    def gate(k):
        return pl.BlockSpec((tm, tn), lambda i, j: (i, gb + k * nj + j))

    return pl.pallas_call(
        _merge_kernel,
        grid=(M // tm, nj),
        in_specs=[br, br, br, gate(0), gate(1), gate(2),
                  pl.BlockSpec((None, 3, MIX, tn), lambda i, j: (l, 0, 0, j))],
        out_specs=pl.BlockSpec((tm, tn), lambda i, j: (i, j)),
        out_shape=jax.ShapeDtypeStruct((M, D_MODEL), BF16),
        compiler_params=_cparams("parallel", "parallel"),
        name="merge_branches",
    )(oa, ob, oc, z, z, z, wb_all)


def _mlstm_kernel(qk_ref, halo_ref, cbuf_ref, v_ref, om_ref, gc_ref, gt_ref, cw_ref, bcol_ref, brow_ref,
                  ng_ref, C0_ref, n0_ref, m0_ref,
                  oa_ref, C_ref, n_ref, m_ref, C_s, n_s, m_s, *, t_valid, lp, nc):
    c = pl.program_id(1)
    rows = qk_ref.shape[0]

    @pl.when(c == 0)
    def _():
        C_s[...] = C0_ref[...]
        n_s[...] = n0_ref[...]
        m_s[...] = m0_ref[...]

    u = _pad_rows(qk_ref[...], lp)
    halo = jnp.where(c == 0, cbuf_ref[...], halo_ref[...])
    cw = cw_ref[...]
    row8 = lax.broadcasted_iota(jnp.int32, (8, 1), 0)
    y = None
    for k in range(ML_CONV - 1, 0, -1):
        ur = pltpu.roll(u, k, 0)
        hr = pltpu.roll(halo, k, 0)
        first = jnp.where(row8 < k, hr, ur[:8])
        sh = jnp.concatenate([first, ur[8:]], axis=0)
        term = sh * cw[ML_CONV - 1 - k:ML_CONV - k]
        y = term if y is None else y + term
    y = y + u * cw[ML_CONV - 1:ML_CONV]
    act = y * _sigmoid(y)
    q = act[:, :MIX]
    k_all = act[:, MIX:] * (ML_DK ** -0.5)
    v = _pad_rows(v_ref[...], lp)
    om = om_ref[...]

    rowi = lax.broadcasted_iota(jnp.int32, (lp, lp), 0)
    coli = lax.broadcasted_iota(jnp.int32, (lp, lp), 1)
    causal = rowi >= coli
    tril = causal.astype(F32)
    triu = (rowi <= coli).astype(F32)
    valid_c = lax.broadcasted_iota(jnp.int32, (lp, 1), 0) < t_valid
    valid_r = lax.broadcasted_iota(jnp.int32, (1, lp), 1) < t_valid
    pre_c = _pad_rows(gc_ref[...], lp) + brow_ref[...]
    li_c = jnp.where(valid_c, pre_c, NEG)
    lf_c = jnp.where(valid_c, -_softplus(-pre_c), 0.0)
    b_c = jnp.dot(tril, lf_c, precision=HIGHEST, preferred_element_type=F32)
    pre_r = gt_ref[...] + bcol_ref[...]
    li_r = jnp.where(valid_r, pre_r, NEG)
    lf_r = jnp.where(valid_r, -_softplus(-pre_r), 0.0)
    b_r = jnp.dot(lf_r, triu, precision=HIGHEST, preferred_element_type=F32)

    ng = ng_ref[...]
    for h in range(ML_HEADS):
        sl = slice(h * ML_DK, (h + 1) * ML_DK)
        bc = b_c[:, ML_HEADS + h:ML_HEADS + h + 1]
        br = b_r[ML_HEADS + h:ML_HEADS + h + 1, :]
        lir = li_r[h:h + 1, :]
        lic = li_c[:, h:h + 1]
        m_prev = m_s[h:h + 1, 0:1]
        qh = q[:, sl]
        kh = k_all[:, sl]
        vh = v[:, sl].astype(BF16)
        qb = qh.astype(BF16)
        log_d = jnp.where(causal, bc - br + lir, NEG)
        log_inter = bc + m_prev
        m_t = jnp.maximum(log_inter, jnp.max(log_d, axis=-1, keepdims=True))
        s = _dot_nt(qb, kh.astype(BF16)) * jnp.exp(log_d - m_t)
        inter = jnp.exp(log_inter - m_t)
        C_h = C_s[h]
        num = _dot(s.astype(BF16), vh) + _dot(qb, C_h.astype(BF16)) * inter
        den = (jnp.sum(s, axis=-1, keepdims=True)
               + jnp.sum(qh * n_s[h:h + 1, :], axis=-1, keepdims=True) * inter)
        hh = num / jnp.maximum(jnp.abs(den), jnp.exp(-m_t))
        b_last = bc[lp - 1:lp, :]
        log_w_r = b_last - br + lir
        m_new = jnp.maximum(b_last + m_prev, jnp.max(log_w_r, axis=-1, keepdims=True))
        w_c = jnp.exp(b_last - bc + lic - m_new)
        dec = jnp.exp(b_last + m_prev - m_new)
        kw = kh * w_c
        C_s[h] = dec * C_h + _dot_tn(kw.astype(BF16), vh)
        n_s[h:h + 1, :] = dec * n_s[h:h + 1, :] + jnp.sum(kw, axis=0, keepdims=True)
        m_s[h:h + 1, :] = jnp.broadcast_to(m_new, (1, 128))
        mu = jnp.mean(hh, axis=-1, keepdims=True)
        xc = hh - mu
        var = jnp.mean(xc * xc, axis=-1, keepdims=True)
        hn = xc * lax.rsqrt(var + NORM_EPS) * ng[:, sl]
        oa_ref[:, sl] = _sigmoid(om[:, sl]) * hn[:rows]

    @pl.when(c == nc - 1)
    def _():
        C_ref[...] = C_s[...]
        n_ref[...] = n_s[...]
        m_ref[...] = m_s[...]


def mlstm_branch(z, gt, B, T, lc, conv_buf8, conv_w, bias_col, bias_row, norm_g, C0, n0, m0):
    nc = T // lc
    lp = gt.shape[-1] // nc
    lc8 = lc // 8
    kern = functools.partial(_mlstm_kernel, t_valid=lc, lp=lp, nc=nc)
    st = lambda b, c: (b, 0, 0)
    out = pl.pallas_call(
        kern,
        grid=(B, nc),
        in_specs=[
            pl.BlockSpec((lc, 2 * MIX), lambda b, c: (b * nc + c, 0)),
            pl.BlockSpec((8, 2 * MIX), lambda b, c: (jnp.maximum((b * nc + c) * lc8 - 1, 0), 0)),
            pl.BlockSpec((None, 8, 2 * MIX), lambda b, c: (b, 0, 0)),
            pl.BlockSpec((lc, MIX), lambda b, c: (b * nc + c, 2)),
            pl.BlockSpec((lc, MIX), lambda b, c: (b * nc + c, 3)),
            pl.BlockSpec((lc, 128), lambda b, c: (b * nc + c, Z_IF // 128)),
            pl.BlockSpec((None, 8, lp), lambda b, c: (b, 0, c)),
            pl.BlockSpec((ML_CONV, 2 * MIX), lambda b, c: (0, 0)),
            pl.BlockSpec((8, 1), lambda b, c: (0, 0)),
            pl.BlockSpec((1, 128), lambda b, c: (0, 0)),
            pl.BlockSpec((1, MIX), lambda b, c: (0, 0)),
            pl.BlockSpec((None, ML_HEADS, ML_DK, ML_DK), lambda b, c: (b, 0, 0, 0)),
            pl.BlockSpec((None, ML_HEADS, ML_DK), st),
            pl.BlockSpec((None, ML_HEADS, 128), st),
        ],
        out_specs=[
            pl.BlockSpec((lc, MIX), lambda b, c: (b * nc + c, 0)),
            pl.BlockSpec((None, ML_HEADS, ML_DK, ML_DK), lambda b, c: (b, 0, 0, 0)),
            pl.BlockSpec((None, ML_HEADS, ML_DK), st),
            pl.BlockSpec((None, ML_HEADS, 128), st),
        ],
        out_shape=[
            jax.ShapeDtypeStruct((B * T, MIX), F32),
            jax.ShapeDtypeStruct((B, ML_HEADS, ML_DK, ML_DK), F32),
            jax.ShapeDtypeStruct((B, ML_HEADS, ML_DK), F32),
            jax.ShapeDtypeStruct((B, ML_HEADS, 128), F32),
        ],
        scratch_shapes=[pltpu.VMEM((ML_HEADS, ML_DK, ML_DK), F32),
                        pltpu.VMEM((ML_HEADS, ML_DK), F32),
                        pltpu.VMEM((ML_HEADS, 128), F32)],
        compiler_params=_cparams("parallel", "arbitrary"),
        name="mlstm",
    )(z, z, conv_buf8, z, z, z, gt, conv_w, bias_col, bias_row, norm_g, C0, n0, m0)
    return out


def _rope_kernel(q_ref, k_ref, v_ref, cos_ref, sin_ref, qo_ref, ko_ref, kb_ref, vo_ref, vb_ref):
    cos = cos_ref[...]
    sin = sin_ref[...]
    lane = lax.broadcasted_iota(jnp.int32, (1, MIX), 1)
    lo = (lane % DA_HD) < (DA_HD // 2)

    def rot(x):
        partner = jnp.where(lo, pltpu.roll(x, MIX - DA_HD // 2, 1), pltpu.roll(x, DA_HD // 2, 1))
        return x * cos + partner * sin

    qo_ref[...] = (rot(q_ref[...]) * (DA_HD ** -0.5)).astype(qo_ref.dtype)
    kr = rot(k_ref[...])
    kb_ref[...] = kr.astype(BF16)
    v = v_ref[...]
    vb_ref[...] = v.astype(BF16)
    tm = v.shape[0]
    for h in range(DA_HEADS):
        rows_h = pl.ds(h, tm, stride=DA_HEADS)
        ko_ref[rows_h, :] = kr[:, h * DA_VD:(h + 1) * DA_VD]
        vo_ref[rows_h, :] = v[:, h * DA_VD:(h + 1) * DA_VD]


def rope_qkv(z, cos, sin, tm, q_dtype):
    M = z.shape[0]
    nt = cos.shape[0] // tm
    cb = Z_DA // MIX
    row = lambda i: (i, 0)
    tab = pl.BlockSpec((tm, MIX), lambda i: (i % nt, 0))
    blk = pl.BlockSpec((tm, MIX), row)
    blk4 = pl.BlockSpec((tm * DA_HEADS, DA_VD), row)
    rows4 = jax.ShapeDtypeStruct((M * DA_HEADS, DA_VD), F32)
    return pl.pallas_call(
        _rope_kernel,
        grid=(M // tm,),
        in_specs=[pl.BlockSpec((tm, MIX), lambda i: (i, cb)),
                  pl.BlockSpec((tm, MIX), lambda i: (i, cb + 1)),
                  pl.BlockSpec((tm, MIX), lambda i: (i, cb + 2)), tab, tab],
        out_specs=[blk, blk4, blk, blk4, blk],
        out_shape=[jax.ShapeDtypeStruct((M, MIX), q_dtype), rows4,
                   jax.ShapeDtypeStruct((M, MIX), BF16), rows4,
                   jax.ShapeDtypeStruct((M, MIX), BF16)],
        compiler_params=_cparams("parallel"),
        name="rope_qkv",
    )(z, z, z, cos, sin)


def _softmax_step(s, v, m_ref, l_ref, acc_ref, idx):
    m_prev = m_ref[idx]
    m_new = jnp.maximum(m_prev, jnp.max(s, axis=-1, keepdims=True))
    alpha = jnp.exp(m_prev - m_new)
    p = jnp.exp(s - m_new[:, :1])
    l_ref[idx] = alpha * l_ref[idx] + jnp.sum(p, axis=-1, keepdims=True)
    acc_ref[idx] = alpha * acc_ref[idx] + _dot(p.astype(BF16), v)
    m_ref[idx] = m_new


def _subln(o, g, scale):
    ms = jnp.mean(o * o, axis=-1, keepdims=True)
    return o * lax.rsqrt(ms + NORM_EPS) * g * scale


def _flash_kernel(lam_ref, q_ref, k_ref, v_ref, g_ref, o_ref, m_s, l_s, acc_s, *, tq, out_scale):
    i = pl.program_id(2)
    q = q_ref[...]
    lane = lax.broadcasted_iota(jnp.int32, (1, DA_VD), 1)
    zero = jnp.zeros_like(q)
    qs = (jnp.where(lane < DA_HD, q, zero), jnp.where(lane >= DA_HD, q, zero))
    m_s[...] = jnp.full(m_s.shape, NEG, F32)
    l_s[...] = jnp.zeros(l_s.shape, F32)
    acc_s[...] = jnp.zeros(acc_s.shape, F32)
    causal = (lax.broadcasted_iota(jnp.int32, (tq, tq), 0) >= lax.broadcasted_iota(jnp.int32, (tq, tq), 1))

    def step(j, masked):
        off = pl.multiple_of(j * tq, tq)
        k = k_ref[pl.ds(off, tq), :]
        v = v_ref[pl.ds(off, tq), :]
        for c in range(2):
            s = _dot_nt(qs[c], k)
            if masked:
                s = jnp.where(causal, s, NEG)
            _softmax_step(s, v, m_s, l_s, acc_s, c)

    def body(j, carry):
        step(j, False)
        return carry

    lax.fori_loop(0, i, body, 0)
    step(i, True)
    o = acc_s[0] / l_s[0] - lam_ref[0] * (acc_s[1] / l_s[1])
    o_ref[...] = _subln(o, g_ref[...], out_scale)


def flash_diff_attention(lam, qb, kb, vb, sub_g, B, T, tq, out_scale):
    nq = T // tq
    kern = functools.partial(_flash_kernel, tq=tq, out_scale=out_scale)
    kv = pl.BlockSpec((T, DA_VD), lambda b, h, i: (b, h))
    return pl.pallas_call(
        kern,
        grid=(B, DA_HEADS, nq),
        in_specs=[pl.BlockSpec(memory_space=pltpu.SMEM),
                  pl.BlockSpec((tq, DA_VD), lambda b, h, i: (b * nq + i, h)), kv, kv,
                  pl.BlockSpec((1, DA_VD), lambda b, h, i: (0, 0))],
        out_specs=pl.BlockSpec((tq, DA_VD), lambda b, h, i: (b * nq + i, h)),
        out_shape=jax.ShapeDtypeStruct((B * T, MIX), F32),
        scratch_shapes=[pltpu.VMEM((2, tq, DA_VD), F32), pltpu.VMEM((2, tq, DA_VD), F32),
                        pltpu.VMEM((2, tq, DA_VD), F32)],
        compiler_params=_cparams("parallel", "parallel", "arbitrary"),
        name="flash_diff_attn",
    )(lam, qb, kb, vb, sub_g)


PAGES_PER_STEP = 4


def _dec_attn_kernel(pt_ref, lam_ref, q_ref, *refs, n_steps, tnew, out_scale):
    P = PAGES_PER_STEP
    k_refs, v_refs = refs[:P], refs[P:2 * P]
    kn_ref, vn_ref, g_ref, o_ref, w_s, m_s, l_s, acc_s = refs[2 * P:]
    j = pl.program_id(1)
    nl = DA_HEADS * 2 * tnew
    lane = lax.broadcasted_iota(jnp.int32, (1, nl), 1)
    lane_head = lane // (2 * tnew)

    @pl.when(j == 0)
    def _():
        q = q_ref[...]
        dl = lax.broadcasted_iota(jnp.int32, (1, DA_VD), 1)
        rows = []
        for h in range(DA_HEADS):
            qh = q[:, h * DA_VD:(h + 1) * DA_VD]
            rows += [jnp.where(dl < DA_HD, qh, 0.0), jnp.where(dl >= DA_HD, qh, 0.0)]
        w_s[...] = jnp.concatenate(rows, axis=0).T.astype(BF16)
        m_s[...] = jnp.full(m_s.shape, NEG, F32)
        l_s[...] = jnp.zeros(l_s.shape, F32)
        acc_s[...] = jnp.zeros(acc_s.shape, F32)

    def attend(k_of, v_of, n_blk, mask):
        w = w_s[...]
        blocks = []
        for p in range(n_blk):
            s_p = None
            for h in range(DA_HEADS):
                sh = _dot(k_of(p, h), w)
                s_p = sh if s_p is None else jnp.where(lane_head == h, sh, s_p)
            blocks.append(s_p)
        s = blocks[0] if n_blk == 1 else jnp.concatenate(blocks, axis=0)
        if mask is not None:
            s = jnp.where(mask, s, NEG)
        m_prev = m_s[0:1, :]
        m_new = jnp.maximum(m_prev, jnp.max(s, axis=0, keepdims=True))
        alpha = jnp.exp(m_prev - m_new)
        pr = jnp.exp(s - m_new)
        l_s[...] = jnp.broadcast_to(alpha * l_s[0:1, :] + jnp.sum(pr, axis=0, keepdims=True), l_s.shape)
        m_s[...] = jnp.broadcast_to(m_new, m_s.shape)
        pt = pr.T.astype(BF16)
        alpha_col = jnp.broadcast_to(alpha, (nl, nl)).T
        pv = []
        for h in range(DA_HEADS):
            r0 = h * 2 * tnew
            acc_h = None
            for p in range(n_blk):
                d = _dot(pt[r0:r0 + 2 * tnew, p * PAGE_SIZE:(p + 1) * PAGE_SIZE], v_of(p, h))
                acc_h = d if acc_h is None else acc_h + d
            pv.append(acc_h)
        acc_s[...] = alpha_col * acc_s[...] + jnp.concatenate(pv, axis=0)

    @pl.when(j < n_steps)
    def _():
        rows_h = lambda h: pl.ds(h, PAGE_SIZE, stride=DA_HEADS)
        attend(lambda p, h: k_refs[p][rows_h(h), :].astype(BF16),
               lambda p, h: v_refs[p][rows_h(h), :].astype(BF16), P, None)

    @pl.when(j == n_steps)
    def _():
        ti = lax.broadcasted_iota(jnp.int32, (PAGE_SIZE, nl), 0)
        qi = lax.broadcasted_iota(jnp.int32, (PAGE_SIZE, nl), 1) % tnew
        new_h = lambda h: pl.ds(h, tnew, stride=DA_HEADS)
        attend(lambda p, h: _pad_rows(kn_ref[new_h(h), :], PAGE_SIZE).astype(BF16),
               lambda p, h: _pad_rows(vn_ref[new_h(h), :], PAGE_SIZE).astype(BF16), 1, ti <= qi)
        lam = lam_ref[0]
        g = g_ref[...]
        l_col = jnp.broadcast_to(l_s[0:1, :], (nl, nl)).T
        a = acc_s[...] / l_col
        for h in range(DA_HEADS):
            r0 = h * 2 * tnew
            o = a[r0:r0 + tnew] - lam * a[r0 + tnew:r0 + 2 * tnew]
            o_ref[:, h * DA_VD:(h + 1) * DA_VD] = _subln(o, g, out_scale)


def paged_diff_attention(page_table, lam, q, cache_k, cache_v, l, k_new, v_new, sub_g, B, tnew, out_scale):
    P = PAGES_PER_STEP
    n_pages = page_table.shape[1]
    assert n_pages % P == 0 and DA_HEADS * 2 * tnew == 128
    n_steps = n_pages // P
    kern = functools.partial(_dec_attn_kernel, n_steps=n_steps, tnew=tnew, out_scale=out_scale)

    def page(p):
        return pl.BlockSpec((None, None, PAGE_SIZE * DA_HEADS, DA_VD),
                            lambda b, j, pt: (l, pt[b, jnp.minimum(j * P + p, n_pages - 1)], 0, 0))

    cache_k = cache_k.reshape(cache_k.shape[:2] + (PAGE_SIZE * DA_HEADS, DA_VD))
    cache_v = cache_v.reshape(cache_v.shape[:2] + (PAGE_SIZE * DA_HEADS, DA_VD))
    rows = pl.BlockSpec((tnew, MIX), lambda b, j, pt: (b, 0))
    new = pl.BlockSpec((tnew * DA_HEADS, DA_VD), lambda b, j, pt: (b, 0))
    nl = DA_HEADS * 2 * tnew
    grid_spec = pltpu.PrefetchScalarGridSpec(
        num_scalar_prefetch=1,
        grid=(B, n_steps + 1),
        in_specs=[pl.BlockSpec(memory_space=pltpu.SMEM), rows] + [page(p) for p in range(P)] * 2
                 + [new, new, pl.BlockSpec((1, DA_VD), lambda b, j, pt: (0, 0))],
        out_specs=rows,
        scratch_shapes=[pltpu.VMEM((DA_VD, nl), BF16), pltpu.VMEM((8, nl), F32), pltpu.VMEM((8, nl), F32),
                        pltpu.VMEM((nl, DA_VD), F32)],
    )
    return pl.pallas_call(
        kern,
        grid_spec=grid_spec,
        out_shape=jax.ShapeDtypeStruct((B * tnew, MIX), F32),
        compiler_params=_cparams("parallel", "arbitrary"),
        name="paged_diff_attn",
    )(page_table, lam, q, *([cache_k] * P), *([cache_v] * P), k_new, v_new, sub_g)


def _seg_sum(x, ones_bd):
    hi = x.astype(BF16)
    lo = (x - hi.astype(F32)).astype(BF16)
    return _dot(hi, ones_bd) + _dot(lo, ones_bd)


def _rwkv_prep_kernel(x_ref, halo_ref, shift_ref, mu_ref, w0_ref, a0_ref, kkw_ref, ka_ref, w2_ref, a2_ref,
                      g2_ref, ones_ref, r_ref, lw_ref, k_ref, v_ref, av_ref, bv_ref, g_ref, *, nt):
    i = pl.program_id(0)
    x = x_ref[...]
    tm = x.shape[0]
    prev_row = jnp.where(i % nt == 0, shift_ref[...], halo_ref[7:8, :])
    row = lax.broadcasted_iota(jnp.int32, (tm, 1), 0)
    prev = jnp.where(row == 0, prev_row, pltpu.roll(x, 1, 0))
    mixed = x + (prev - x) * mu_ref[...]
    r = mixed[:, :MIX]
    kr = mixed[:, MIX:2 * MIX]
    vr = mixed[:, 2 * MIX:3 * MIX]
    o = 3 * MIX
    wd = mixed[:, o:o + RW_LORA_PAD]
    ad = mixed[:, o + RW_LORA_PAD:o + 2 * RW_LORA_PAD]
    gd = mixed[:, o + 2 * RW_LORA_PAD:]
    wl = w0_ref[...] + _dot(jnp.tanh(wd).astype(BF16), w2_ref[...])
    w_log = -_softplus(-wl) - 0.5
    a = _sigmoid(a0_ref[...] + _dot(ad.astype(BF16), a2_ref[...]))
    kk = kr * kkw_ref[...]
    ss = _seg_sum(kk * kk, ones_ref[...])
    kk = kk * lax.rsqrt(jnp.maximum(ss, 1e-24))
    r_ref[...] = r
    lw_ref[...] = -jnp.exp(w_log)
    k_ref[...] = kr * (1.0 + (a - 1.0) * ka_ref[...])
    v_ref[...] = vr
    av_ref[...] = -kk
    bv_ref[...] = kk * a
    g_ref[...] = _dot(_sigmoid(gd).astype(BF16), g2_ref[...])


def rwkv_prep(z, shift_pad, nt, tm, mu, w0, a0, kkw, ka, w2, a2, g2, ones_bd):
    M = z.shape[0]
    cb = Z_RW // RW_W_PAD
    tm8 = tm // 8
    vec = pl.BlockSpec((1, MIX), lambda i: (0, 0))
    out = pl.BlockSpec((tm, MIX), lambda i: (i, 0))
    kern = functools.partial(_rwkv_prep_kernel, nt=nt)
    return pl.pallas_call(
        kern,
        grid=(M // tm,),
        in_specs=[pl.BlockSpec((tm, RW_W_PAD), lambda i: (i, cb)),
                  pl.BlockSpec((8, RW_W_PAD), lambda i: (jnp.maximum(i * tm8 - 1, 0), cb)),
                  pl.BlockSpec((None, 1, RW_W_PAD), lambda i: (i // nt, 0, 0)),
                  pl.BlockSpec((1, RW_W_PAD), lambda i: (0, 0)),
                  vec, vec, vec, vec,
                  pl.BlockSpec((RW_LORA_PAD, MIX), lambda i: (0, 0)),
                  pl.BlockSpec((RW_LORA_PAD, MIX), lambda i: (0, 0)),
                  pl.BlockSpec((RW_GATE, MIX), lambda i: (0, 0)),
                  pl.BlockSpec((MIX, MIX), lambda i: (0, 0))],
        out_specs=[out] * 7,
        out_shape=[jax.ShapeDtypeStruct((M, MIX), F32)] * 7,
        compiler_params=_cparams("parallel"),
        name="rwkv_prep",
    )(z, z, shift_pad, mu, w0, a0, kkw, ka, w2, a2, g2, ones_bd)


def _rwkv_chunk_kernel(r_ref, lw_ref, k_ref, v_ref, av_ref, bv_ref,
                       ah_ref, x_ref, rt_ref, arb_ref, yv_ref, bh_ref, z_ref, gc_ref):
    L = RW_CHUNK
    lw = _pad_rows(lw_ref[...], L)
    rowi = lax.broadcasted_iota(jnp.int32, (L, 1), 0)
    c = lw
    for s in (1, 2, 4, 8, 16, 32):
        c = c + jnp.where(rowi >= s, pltpu.roll(c, s, 0), 0.0)
    c_last = c[L - 1:L, :]
    e_neg = jnp.exp(-c)
    e_last = jnp.exp(c_last - c)
    av = _pad_rows(av_ref[...], L)
    bv = _pad_rows(bv_ref[...], L)
    kv = _pad_rows(k_ref[...], L)
    vv = _pad_rows(v_ref[...], L)
    at = av * jnp.exp(c - lw)
    rt = _pad_rows(r_ref[...], L) * jnp.exp(c)
    bt = bv * e_neg
    kt = kv * e_neg
    bh = bv * e_last
    kh = kv * e_last
    rt_ref[...] = rt.astype(BF16)
    bh_ref[...] = bh.astype(BF16)

    lane = lax.broadcasted_iota(jnp.int32, (1, 128), 1)
    m0 = lane < RW_HEAD
    r2 = lax.broadcasted_iota(jnp.int32, (128, 128), 0)
    c2 = lax.broadcasted_iota(jnp.int32, (128, 128), 1)
    same = (r2 // RW_HEAD) == (c2 // RW_HEAD)
    mask_sl = same & ((r2 % RW_HEAD) > (c2 % RW_HEAD))
    mask_l = same & ((r2 % RW_HEAD) >= (c2 % RW_HEAD))
    eye = (r2 == c2).astype(F32)
    g_last = jnp.exp(c_last)

    def dot3(a, b):
        ah, bh_ = a.astype(BF16), b.astype(BF16)
        al = (a - ah.astype(F32)).astype(BF16)
        bl = (b - bh_.astype(F32)).astype(BF16)
        return _dot(ah, bh_) + (_dot(ah, bl) + _dot(al, bh_))

    def split(x):
        return jnp.concatenate([jnp.where(m0, x, 0.0), jnp.where(m0, 0.0, x)], axis=0)

    def fold(x):
        return x[:L] + x[L:]

    pairs = range(RW_PAIRS)
    sls = [slice(p * 128, (p + 1) * 128) for p in pairs]
    a_st, v_st, n_bd, a_ak = [], [], [], []
    for p in pairs:
        sl = sls[p]
        a_st.append(split(at[:, sl]).astype(BF16))
        v_st.append(split(vv[:, sl]).astype(BF16))
        r_st = split(rt[:, sl]).astype(BF16)
        b2 = bt[:, sl].astype(BF16)
        k2 = kt[:, sl].astype(BF16)
        lhs = jnp.concatenate([a_st[p], r_st], axis=0)
        rhs = jnp.concatenate([b2, b2, k2, k2], axis=0)
        G = _dot_nt(lhs, rhs)
        n_bd.append(jnp.where(mask_sl, G[:128, :128], 0.0))
        a_ak.append(jnp.where(mask_sl, G[:128, 128:], 0.0))
        arb_ref[:, sl] = fold(jnp.where(mask_l, G[128:, :128], 0.0)).astype(BF16)
        yv_ref[:, sl] = fold(_dot(jnp.where(mask_l, G[128:, 128:], 0.0).astype(BF16), v_st[p]))
        z_ref[:, sl] = jnp.where(same, _dot_tn(kh[:, sl].astype(BF16), vv[:, sl].astype(BF16)), 0.0)
        gc_ref[:, sl] = jnp.broadcast_to(g_last[:, sl], (128, 128)).T

    base = (r2 // 16) == (c2 // 16)
    Mx = [jnp.where(base, n_bd[p], 0.0) for p in pairs]
    D = [eye + Mx[p] for p in pairs]
    for _ in range(3):
        Mx = [dot3(Mx[p], Mx[p]) for p in pairs]
        D = [D[p] + dot3(D[p], Mx[p]) for p in pairs]
    for m in (16, 32):
        off = ((r2 // (2 * m)) == (c2 // (2 * m))) & ((r2 // m) % 2 == 1) & ((c2 // m) % 2 == 0)
        dn = [_dot(D[p].astype(BF16), jnp.where(off, n_bd[p], 0.0).astype(BF16)) for p in pairs]
        D = [D[p] + _dot(dn[p].astype(BF16), D[p].astype(BF16)) for p in pairs]

    Tb = [D[p].astype(BF16) for p in pairs]
    av_st = [_dot(a_ak[p].astype(BF16), v_st[p]).astype(BF16) for p in pairs]
    for p in pairs:
        ah_ref[:, sls[p]] = fold(_dot(Tb[p], a_st[p])).astype(BF16)
        x_ref[:, sls[p]] = fold(_dot(Tb[p], av_st[p]))


def rwkv_chunk(r, lw, k, v, av, bv, rows):
    M = r.shape[0]
    nck = M // rows
    L = RW_CHUNK
    inp = pl.BlockSpec((rows, MIX), lambda i: (i, 0))
    o64 = pl.BlockSpec((L, MIX), lambda i: (i, 0))
    o128 = pl.BlockSpec((128, MIX), lambda i: (i, 0))
    s64 = lambda dt: jax.ShapeDtypeStruct((nck * L, MIX), dt)
    s128 = jax.ShapeDtypeStruct((nck * 128, MIX), F32)
    return pl.pallas_call(
        _rwkv_chunk_kernel,
        grid=(nck,),
        in_specs=[inp] * 6,
        out_specs=[o64, o64, o64, o64, o64, o64, o128, o128],
        out_shape=[s64(BF16), s64(F32), s64(BF16), s64(BF16), s64(F32), s64(BF16), s128, s128],
        compiler_params=_cparams("parallel"),
        name="rwkv_chunk",
    )(r, lw, k, v, av, bv)


def _rwkv_seq_kernel(ah_ref, x_ref, rt_ref, arb_ref, yv_ref, bh_ref, z_ref, gc_ref, h0_ref,
                     y_ref, hout_ref, h_s, *, nb):
    L = RW_CHUNK

    @pl.when(pl.program_id(0) == 0)
    def _():
        h_s[...] = h0_ref[...]

    lane = lax.broadcasted_iota(jnp.int32, (1, 128), 1)
    m0 = lane < RW_HEAD
    r2 = lax.broadcasted_iota(jnp.int32, (128, 128), 0)
    c2 = lax.broadcasted_iota(jnp.int32, (128, 128), 1)
    same = (r2 // RW_HEAD) == (c2 // RW_HEAD)
    for b in range(nb):
        for p in range(RW_PAIRS):
            sl = slice(p * 128, (p + 1) * 128)
            H = h_s[b, p]
            Hb = H.astype(BF16)
            U = _dot(ah_ref[b, :, sl], Hb) + x_ref[b, :, sl]
            u_st = jnp.concatenate([jnp.where(m0, U, 0.0), jnp.where(m0, 0.0, U)], axis=0).astype(BF16)
            y_ref[b, :, sl] = _dot(rt_ref[b, :, sl], Hb) + _dot(arb_ref[b, :, sl], u_st) + yv_ref[b, :, sl]
            upd = _dot_tn(bh_ref[b, :, sl], U.astype(BF16))
            h_s[b, p] = gc_ref[b, :, sl] * H + jnp.where(same, upd, 0.0) + z_ref[b, :, sl]

    hout_ref[...] = h_s[...]


def rwkv_seq(ah, x, rt, arb, yv, bh, zb, gc, h0, B, nc):
    L = RW_CHUNK
    t3 = lambda a, n: a.reshape(B, nc * n, MIX)
    b64 = pl.BlockSpec((B, L, MIX), lambda c: (0, c, 0))
    b128 = pl.BlockSpec((B, 128, MIX), lambda c: (0, c, 0))
    hs = pl.BlockSpec((B, RW_PAIRS, 128, 128), lambda c: (0, 0, 0, 0))
    kern = functools.partial(_rwkv_seq_kernel, nb=B)
    return pl.pallas_call(
        kern,
        grid=(nc,),
        in_specs=[b64, b64, b64, b64, b64, b64, b128, b128, hs],
        out_specs=[b64, hs],
        out_shape=[jax.ShapeDtypeStruct((B, nc * L, MIX), F32),
                   jax.ShapeDtypeStruct((B, RW_PAIRS, 128, 128), F32)],
        scratch_shapes=[pltpu.VMEM((B, RW_PAIRS, 128, 128), F32)],
        compiler_params=_cparams("arbitrary"),
        name="rwkv_seq",
    )(t3(ah, L), t3(x, L), t3(rt, L), t3(arb, L), t3(yv, L), t3(bh, L), t3(zb, 128), t3(gc, 128), h0)


def _rwkv_post_kernel(y_ref, r_ref, k_ref, v_ref, g_ref, gn_g_ref, gn_b_ref, rk_ref, ones_ref, o_ref):
    ones_bd = ones_ref[...]
    y = y_ref[...]
    inv = 1.0 / RW_HEAD
    mu = _seg_sum(y, ones_bd) * inv
    xc = y - mu
    var = _seg_sum(xc * xc, ones_bd) * inv
    yn = xc * lax.rsqrt(var + RW_GN_EPS) * gn_g_ref[...] + gn_b_ref[...]
    bonus = _seg_sum(r_ref[...] * k_ref[...] * rk_ref[...], ones_bd)
    o_ref[...] = (yn + bonus * v_ref[...]) * g_ref[...]


def rwkv_post(y3, r, k, v, g, gn_g, gn_b, rk, ones_bd, B, T, tm):
    nt = T // tm
    t3 = lambda a: a.reshape(B, T, MIX)
    blk = pl.BlockSpec((None, tm, MIX), lambda b, i: (b, i, 0))
    vec = pl.BlockSpec((1, MIX), lambda b, i: (0, 0))
    out = pl.pallas_call(
        _rwkv_post_kernel,
        grid=(B, nt),
        in_specs=[blk, blk, blk, blk, blk, vec, vec, vec, pl.BlockSpec((MIX, MIX), lambda b, i: (0, 0))],
        out_specs=blk,
        out_shape=jax.ShapeDtypeStruct((B, T, MIX), F32),
        compiler_params=_cparams("parallel", "parallel"),
        name="rwkv_post",
    )(y3, t3(r), t3(k), t3(v), t3(g), gn_g, gn_b, rk, ones_bd)
    return out.reshape(B * T, MIX)


def _regroup_w_in(w_in):
    o = 0

    def take(n):
        nonlocal o
        s = w_in[:, :, o:o + n]
        o += n
        return s

    qm, km, vm, om = take(MIX), take(MIX), take(MIX), take(MIX)
    ig, fg = take(ML_HEADS), take(ML_HEADS)
    qd, kd, vd = take(MIX), take(MIX), take(MIX)
    rw = take(RW_W)
    gates = take(3 * D_MODEL)
    zpad = lambda n: jnp.zeros(w_in.shape[:2] + (n,), w_in.dtype)
    rw_main = rw[:, :, :3 * MIX]
    wd = rw[:, :, 3 * MIX:3 * MIX + RW_LORA]
    ad = rw[:, :, 3 * MIX + RW_LORA:3 * MIX + 2 * RW_LORA]
    gd = rw[:, :, 3 * MIX + 2 * RW_LORA:]
    lp = RW_LORA_PAD - RW_LORA
    cols = [qm, km, vm, om, qd, kd, vd, rw_main, wd, zpad(lp), ad, zpad(lp), gd, gates,
            ig, fg, zpad(Z_IF_W - 2 * ML_HEADS)]
    return jnp.concatenate(cols, axis=-1).astype(BF16)


def _pad_rw_row(v):
    lp = RW_LORA_PAD - RW_LORA
    z = jnp.zeros(v.shape[:-1] + (lp,), v.dtype)
    o = 3 * MIX
    return jnp.concatenate([v[..., :o], v[..., o:o + RW_LORA], z, v[..., o + RW_LORA:o + 2 * RW_LORA], z,
                            v[..., o + 2 * RW_LORA:]], axis=-1)


def _unpad_rw_row(v):
    o = 3 * MIX
    return jnp.concatenate([v[..., :o], v[..., o:o + RW_LORA], v[..., o + RW_LORA_PAD:o + RW_LORA_PAD + RW_LORA],
                            v[..., o + 2 * RW_LORA_PAD:]], axis=-1)


def _rope_tables(pos):
    half = DA_HD // 2
    inv = 1.0 / (ROPE_THETA ** (jnp.arange(0, DA_HD, 2, dtype=F32) / DA_HD))
    ang = pos.astype(F32)[:, None] * inv[None, :]
    cos = jnp.cos(ang)
    sin = jnp.sin(ang)
    cos_g = jnp.concatenate([cos, cos], axis=-1)
    sin_g = jnp.concatenate([-sin, sin], axis=-1)
    reps = MIX // DA_HD
    return jnp.tile(cos_g, (1, reps)), jnp.tile(sin_g, (1, reps))


def _pack_rw_state(S):
    B = S.shape[0]
    St = jnp.swapaxes(S, -1, -2).reshape(B, RW_PAIRS, 2, RW_HEAD, RW_HEAD)
    z = jnp.zeros_like(St[:, :, 0])
    top = jnp.concatenate([St[:, :, 0], z], axis=-1)
    bot = jnp.concatenate([z, St[:, :, 1]], axis=-1)
    return jnp.concatenate([top, bot], axis=-2)


def _unpack_rw_state(Hbd):
    B = Hbd.shape[0]
    h0 = Hbd[:, :, :RW_HEAD, :RW_HEAD]
    h1 = Hbd[:, :, RW_HEAD:, RW_HEAD:]
    St = jnp.stack([h0, h1], axis=2).reshape(B, RW_HEADS, RW_HEAD, RW_HEAD)
    return jnp.swapaxes(St, -1, -2)


def _layer(x, l, W, P, grp):
    B, T = grp["B"], grp["T"]
    M = B * T
    tm = grp["tm"]
    z = norm_matmul(x, P["norm1_g"][l][None], W["w_in"], l, tm, 512)

    lc = grp["ml_chunk"]
    lp = grp["ml_pad"]
    g8 = z[:, Z_IF:Z_IF + 8].reshape(B, T, 8)
    gt = jnp.swapaxes(g8, 1, 2)
    if lp != lc:
        gt = jnp.pad(gt, ((0, 0), (0, 0), (0, lp - lc)))
    bias = jnp.concatenate([P["ml_ib"][l], P["ml_fb"][l]])
    bias_row = jnp.pad(bias, (0, 120))[None]
    oa, C, n, m = mlstm_branch(z, gt, B, T, lc, grp["conv8"][l], P["ml_conv_w"][l], bias[:, None], bias_row,
                               P["ml_norm_g"][l][None], grp["C0"][l], grp["n0"][l], grp["m0"][l])
    conv_new = z[:, :2 * MIX].reshape(B, T, 2 * MIX)[:, T - (ML_CONV - 1):]

    qb, k_rot, kb, v_rows, vb = rope_qkv(z, grp["cos"], grp["sin"], grp["rope_tm"],
                                         F32 if grp["paged"] else BF16)
    lam_init = 0.8 - 0.6 * math.exp(-0.3 * l)
    lpar = P["da_lam"][l]
    lam = (jnp.exp(jnp.sum(lpar[0] * lpar[1])) - jnp.exp(jnp.sum(lpar[2] * lpar[3])) + lam_init).reshape(1)
    sub_g = P["da_subln_g"][l][None]
    if grp["paged"]:
        ob = paged_diff_attention(grp["page_table"], lam, qb, grp["cache_k"], grp["cache_v"], l, k_rot, v_rows,
                                  sub_g, B, T, 1.0 - lam_init)
    else:
        ob = flash_diff_attention(lam, qb, kb, vb, sub_g, B, T, grp["tq"], 1.0 - lam_init)

    rtm = grp["rw_tm"]
    r, lw, kmod, vr, av, bv, g = rwkv_prep(z, grp["shift"][l], T // rtm, rtm, W["rw_mu"][l], P["rw_w0"][l][None],
                                           P["rw_a0"][l][None], P["rw_kk"][l][None], P["rw_ka"][l][None],
                                           W["rw_w2"][l], W["rw_a2"][l], W["rw_g2"][l], W["ones_bd"])
    rows = grp["rw_rows"]
    nc = T // rows
    pre = rwkv_chunk(r, lw, kmod, vr, av, bv, rows)
    y3, Hbd = rwkv_seq(*pre, grp["H0"][l], B, nc)
    oc = rwkv_post(y3, r, kmod, vr, g, P["rw_gn_g"][l][None], P["rw_gn_b"][l][None],
                   P["rw_rk"][l].reshape(1, MIX), W["ones_bd"], B, T, grp["post_tm"])
    shift_new = _unpad_rw_row(z[:, Z_RW:Z_RW + RW_W_PAD].reshape(B, T, RW_W_PAD)[:, -1])
    S = _unpack_rw_state(Hbd)

    merged = merge_branches(oa, ob, oc, z, W["w_branch"], l, tm, 512)
    x = matmul_residual(merged, W["w_out"], l, x, tm, 512)
    act = ffn_in(x, P["norm2_g"][l][None], W["w_ffn_in"], l, tm, 512)
    x = matmul_residual(act, W["w_ffn_out"], l, x, grp["ffn_out_tm"], 512)
    outs = (k_rot.reshape(B, T, DA_HEADS, DA_VD), v_rows.reshape(B, T, DA_HEADS, DA_VD), conv_new,
            C, n, m[:, :, 0], shift_new, S)
    return x, outs


def _prep_weights(w_in, w_branch, w_out, w_ffn_in, w_ffn_out, rw_mu, rw_w2, rw_a2, rw_g2):
    lpad = RW_LORA_PAD - RW_LORA
    head = jnp.arange(MIX) // RW_HEAD
    return dict(
        w_in=_regroup_w_in(w_in),
        w_branch=w_branch.astype(BF16),
        w_out=w_out.astype(BF16),
        w_ffn_in=w_ffn_in.astype(BF16),
        w_ffn_out=w_ffn_out.astype(BF16),
        rw_mu=_pad_rw_row(rw_mu)[:, None],
        rw_w2=jnp.pad(rw_w2, ((0, 0), (0, lpad), (0, 0))).astype(BF16),
        rw_a2=jnp.pad(rw_a2, ((0, 0), (0, lpad), (0, 0))).astype(BF16),
        rw_g2=rw_g2.astype(BF16),
        ones_bd=(head[:, None] == head[None, :]).astype(BF16),
    )


def _prompt_group(bp, T, depth):
    zeros = lambda *s: jnp.zeros(s, F32)
    cos, sin = _rope_tables(jnp.arange(T))
    M = bp * T
    return dict(
        B=bp, T=T, tm=min(1024, M), ffn_out_tm=min(512, M), ml_chunk=min(256, T), ml_pad=min(256, T),
        rope_tm=min(512, T), tq=min(512, T), paged=False, rw_tm=min(256, T), rw_rows=RW_CHUNK,
        post_tm=min(512, T), norm_tm=min(512, M), cos=cos, sin=sin,
        conv8=zeros(depth, bp, 8, 2 * MIX), C0=zeros(depth, bp, ML_HEADS, ML_DK, ML_DK),
        n0=zeros(depth, bp, ML_HEADS, ML_DK), m0=zeros(depth, bp, ML_HEADS, 128),
        shift=zeros(depth, bp, 1, RW_W_PAD), H0=zeros(depth, bp, RW_PAIRS, 128, 128),
    )


def _sample_group(bs, ts, depth, cache_k, cache_v, page_table, state_ml_conv, state_ml_C, state_ml_n, state_ml_m,
                  state_rw_shift, state_rw_S):
    past = page_table.shape[1] * PAGE_SIZE
    cos, sin = _rope_tables(jnp.tile(past + jnp.arange(ts), bs))
    M = bs * ts
    return dict(
        B=bs, T=ts, tm=M, ffn_out_tm=M, ml_chunk=ts, ml_pad=RW_CHUNK, rope_tm=M, paged=True,
        rw_tm=ts, rw_rows=ts, post_tm=ts, norm_tm=M, cos=cos, sin=sin,
        page_table=page_table, cache_k=cache_k, cache_v=cache_v,
        conv8=jnp.pad(state_ml_conv, ((0, 0), (0, 0), (8 - (ML_CONV - 1), 0), (0, 0))),
        C0=state_ml_C, n0=state_ml_n, m0=jnp.broadcast_to(state_ml_m[..., None], state_ml_m.shape + (128,)),
        shift=_pad_rw_row(state_rw_shift)[:, :, None],
        H0=jnp.stack([_pack_rw_state(state_rw_S[l]) for l in range(depth)]),
    )


def kernel(x_prompt, x_sample, cache_k, cache_v, page_table, state_ml_conv, state_ml_C, state_ml_n, state_ml_m,
           state_rw_shift, state_rw_S, norm1_g, w_in, ml_conv_w, ml_ib, ml_fb, ml_norm_g, da_lam, da_subln_g,
           rw_mu, rw_w0, rw_w2, rw_a0, rw_a2, rw_g2, rw_kk, rw_ka, rw_rk, rw_gn_g, rw_gn_b, w_branch, w_out,
           norm2_g, w_ffn_in, w_ffn_out, norm_f_g):
    P = dict(norm1_g=norm1_g, ml_conv_w=ml_conv_w, ml_ib=ml_ib, ml_fb=ml_fb, ml_norm_g=ml_norm_g, da_lam=da_lam,
             da_subln_g=da_subln_g, rw_w0=rw_w0, rw_a0=rw_a0, rw_kk=rw_kk, rw_ka=rw_ka, rw_rk=rw_rk,
             rw_gn_g=rw_gn_g, rw_gn_b=rw_gn_b, norm2_g=norm2_g)
    depth = w_in.shape[0]
    W = _prep_weights(w_in, w_branch, w_out, w_ffn_in, w_ffn_out, rw_mu, rw_w2, rw_a2, rw_g2)
    bp, T, _ = x_prompt.shape
    bs, ts, _ = x_sample.shape
    grp_p = _prompt_group(bp, T, depth)
    grp_s = _sample_group(bs, ts, depth, cache_k, cache_v, page_table, state_ml_conv, state_ml_C, state_ml_n,
                          state_ml_m, state_rw_shift, state_rw_S)

    results = []
    for x0, grp in ((x_prompt, grp_p), (x_sample, grp_s)):
        x = x0.reshape(grp["B"] * grp["T"], D_MODEL)
        outs = []
        for l in range(depth):
            x, o = _layer(x, l, W, P, grp)
            outs.append(o)
        y = final_norm(x, norm_f_g[None], grp["norm_tm"]).reshape(x0.shape)
        results.append((y, [jnp.stack([o[i] for o in outs]) for i in range(8)]))

    (y_p, sp), (y_s, ss) = results
    out = [y_p, y_s, sp[0], sp[1], ss[0], ss[1]]
    for i in range(2, 8):
        out += [sp[i], ss[i]]
    return tuple(out)
```

```python
import functools
import math

import jax
import jax.numpy as jnp
from jax import lax
from jax.experimental import pallas as pl
from jax.experimental.pallas import tpu as pltpu

F32 = jnp.float32
BF16 = jnp.bfloat16
HIGHEST = lax.Precision.HIGHEST

D_MODEL = 2048
DEPTH = 4
PAGE_SIZE = 128
MIX = D_MODEL // 2
ML_HEADS = 4
ML_DK = MIX // ML_HEADS
ML_CONV = 4
DA_HD = 64
DA_VD = 2 * DA_HD
DA_HEADS = MIX // DA_VD
ROPE_THETA = 10000.0
RW_HEAD = 64
RW_HEADS = MIX // RW_HEAD
RW_PAIRS = RW_HEADS // 2
RW_LORA = 96
RW_LORA_PAD = 128
RW_GATE = 256
RW_W = 3 * MIX + 2 * RW_LORA + RW_GATE
RW_W_PAD = 3 * MIX + 2 * RW_LORA_PAD + RW_GATE
RW_GN_EPS = 64e-5
D_FF = 5632
NORM_EPS = 1e-6
NEG = -1e30

Z_ML = 0
Z_DA = 4 * MIX
Z_RW = Z_DA + 3 * MIX
Z_GATE = Z_RW + RW_W_PAD
Z_IF = Z_GATE + 3 * D_MODEL
Z_IF_W = 512
N_Z = Z_IF + Z_IF_W

RW_CHUNK = 64
VMEM_LIMIT = 52 * 1024 * 1024


def _cparams(*sem):
    return pltpu.CompilerParams(dimension_semantics=sem, vmem_limit_bytes=VMEM_LIMIT)


def _sigmoid(x):
    return 1.0 / (1.0 + jnp.exp(-x))


def _softplus(x):
    return jnp.maximum(x, 0.0) + jnp.log1p(jnp.exp(-jnp.abs(x)))


def _dot(a, b):
    return jnp.dot(a, b, preferred_element_type=F32)


def _dot_nt(a, b):
    return lax.dot_general(a, b, (((1,), (1,)), ((), ())), preferred_element_type=F32)


def _dot_tn(a, b, precision=None):
    return lax.dot_general(a, b, (((0,), (0,)), ((), ())), preferred_element_type=F32, precision=precision)


def _pad_rows(x, rows):
    if x.shape[0] == rows:
        return x
    return jnp.concatenate([x, jnp.zeros((rows - x.shape[0],) + x.shape[1:], x.dtype)], axis=0)


def _norm_mm_kernel(x_ref, g_ref, w_ref, o_ref, hn_ref):
    @pl.when(pl.program_id(1) == 0)
    def _():
        x = x_ref[...]
        ms = jnp.mean(x * x, axis=-1, keepdims=True)
        hn_ref[...] = (x * lax.rsqrt(ms + NORM_EPS) * g_ref[...]).astype(BF16)

    o_ref[...] = _dot(hn_ref[...], w_ref[...])


def norm_matmul(x, g, w_all, l, tm, tn):
    M, D = x.shape
    N = w_all.shape[-1]
    return pl.pallas_call(
        _norm_mm_kernel,
        grid=(M // tm, N // tn),
        in_specs=[pl.BlockSpec((tm, D), lambda i, j: (i, 0)),
                  pl.BlockSpec((1, D), lambda i, j: (0, 0)),
                  pl.BlockSpec((None, D, tn), lambda i, j: (l, 0, j))],
        out_specs=pl.BlockSpec((tm, tn), lambda i, j: (i, j)),
        out_shape=jax.ShapeDtypeStruct((M, N), F32),
        scratch_shapes=[pltpu.VMEM((tm, D), BF16)],
        compiler_params=_cparams("parallel", "arbitrary"),
        name="norm_matmul",
    )(x, g, w_all)


def _ffn_in_kernel(x_ref, g_ref, wg_ref, wu_ref, o_ref, hn_ref):
    @pl.when(pl.program_id(1) == 0)
    def _():
        x = x_ref[...]
        ms = jnp.mean(x * x, axis=-1, keepdims=True)
        hn_ref[...] = (x * lax.rsqrt(ms + NORM_EPS) * g_ref[...]).astype(BF16)

    h = hn_ref[...]
    gf = _dot(h, wg_ref[...])
    uf = _dot(h, wu_ref[...])
    o_ref[...] = (gf * _sigmoid(gf) * uf).astype(BF16)


def ffn_in(x, g, w_all, l, tm, tn):
    M, D = x.shape
    nj = D_FF // tn
    return pl.pallas_call(
        _ffn_in_kernel,
        grid=(M // tm, nj),
        in_specs=[pl.BlockSpec((tm, D), lambda i, j: (i, 0)),
                  pl.BlockSpec((1, D), lambda i, j: (0, 0)),
                  pl.BlockSpec((None, D, tn), lambda i, j: (l, 0, j)),
                  pl.BlockSpec((None, D, tn), lambda i, j: (l, 0, j + nj))],
        out_specs=pl.BlockSpec((tm, tn), lambda i, j: (i, j)),
        out_shape=jax.ShapeDtypeStruct((M, D_FF), BF16),
        scratch_shapes=[pltpu.VMEM((tm, D), BF16)],
        compiler_params=_cparams("parallel", "arbitrary"),
        name="ffn_in",
    )(x, g, w_all, w_all)


def _mm_res_kernel(a_ref, w_ref, r_ref, o_ref):
    o_ref[...] = r_ref[...] + _dot(a_ref[...], w_ref[...])


def matmul_residual(a, w_all, l, res, tm, tn):
    M, K = a.shape
    N = w_all.shape[-1]
    return pl.pallas_call(
        _mm_res_kernel,
        grid=(M // tm, N // tn),
        in_specs=[pl.BlockSpec((tm, K), lambda i, j: (i, 0)),
                  pl.BlockSpec((None, K, tn), lambda i, j: (l, 0, j)),
                  pl.BlockSpec((tm, tn), lambda i, j: (i, j))],
        out_specs=pl.BlockSpec((tm, tn), lambda i, j: (i, j)),
        out_shape=jax.ShapeDtypeStruct((M, N), F32),
        compiler_params=_cparams("parallel", "parallel"),
        name="matmul_residual",
    )(a, w_all, res)


def _norm_kernel(x_ref, g_ref, o_ref):
    x = x_ref[...]
    ms = jnp.mean(x * x, axis=-1, keepdims=True)
    o_ref[...] = x * lax.rsqrt(ms + NORM_EPS) * g_ref[...]


def final_norm(x, g, tm):
    M, D = x.shape
    return pl.pallas_call(
        _norm_kernel,
        grid=(M // tm,),
        in_specs=[pl.BlockSpec((tm, D), lambda i: (i, 0)), pl.BlockSpec((1, D), lambda i: (0, 0))],
        out_specs=pl.BlockSpec((tm, D), lambda i: (i, 0)),
        out_shape=jax.ShapeDtypeStruct((M, D), F32),
        compiler_params=_cparams("parallel"),
        name="final_norm",
    )(x, g)


def _merge_kernel(a_ref, b_ref, c_ref, g0_ref, g1_ref, g2_ref, w_ref, o_ref):
    acc = _sigmoid(g0_ref[...]) * _dot(a_ref[...].astype(BF16), w_ref[0])
    acc = acc + _sigmoid(g1_ref[...]) * _dot(b_ref[...].astype(BF16), w_ref[1])
    acc = acc + _sigmoid(g2_ref[...]) * _dot(c_ref[...].astype(BF16), w_ref[2])
    o_ref[...] = acc.astype(BF16)


def merge_branches(oa, ob, oc, z, wb_all, l, tm, tn):
    M = oa.shape[0]
    nj = D_MODEL // tn
    gb = Z_GATE // tn
    br = pl.BlockSpec((tm, MIX), lambda i, j: (i, 0))

    def gate(k):
        return pl.BlockSpec((tm, tn), lambda i, j: (i, gb + k * nj + j))

    return pl.pallas_call(
        _merge_kernel,
        grid=(M // tm, nj),
        in_specs=[br, br, br, gate(0), gate(1), gate(2),
                  pl.BlockSpec((None, 3, MIX, tn), lambda i, j: (l, 0, 0, j))],
        out_specs=pl.BlockSpec((tm, tn), lambda i, j: (i, j)),
        out_shape=jax.ShapeDtypeStruct((M, D_MODEL), BF16),
        compiler_params=_cparams("parallel", "parallel"),
        name="merge_branches",
    )(oa, ob, oc, z, z, z, wb_all)


def _mlstm_kernel(qk_ref, halo_ref, cbuf_ref, v_ref, om_ref, gc_ref, gt_ref, cw_ref, bcol_ref, brow_ref,
                  ng_ref, C0_ref, n0_ref, m0_ref,
                  oa_ref, C_ref, n_ref, m_ref, C_s, n_s, m_s, *, t_valid, lp, nc):
    c = pl.program_id(1)
    rows = qk_ref.shape[0]

    @pl.when(c == 0)
    def _():
        C_s[...] = C0_ref[...]
        n_s[...] = n0_ref[...]
        m_s[...] = m0_ref[...]

    u = _pad_rows(qk_ref[...], lp)
    halo = jnp.where(c == 0, cbuf_ref[...], halo_ref[...])
    cw = cw_ref[...]
    row8 = lax.broadcasted_iota(jnp.int32, (8, 1), 0)
    y = None
    for k in range(ML_CONV - 1, 0, -1):
        ur = pltpu.roll(u, k, 0)
        hr = pltpu.roll(halo, k, 0)
        first = jnp.where(row8 < k, hr, ur[:8])
        sh = jnp.concatenate([first, ur[8:]], axis=0)
        term = sh * cw[ML_CONV - 1 - k:ML_CONV - k]
        y = term if y is None else y + term
    y = y + u * cw[ML_CONV - 1:ML_CONV]
    act = y * _sigmoid(y)
    q = act[:, :MIX]
    k_all = act[:, MIX:] * (ML_DK ** -0.5)
    v = _pad_rows(v_ref[...], lp)
    om = om_ref[...]

    rowi = lax.broadcasted_iota(jnp.int32, (lp, lp), 0)
    coli = lax.broadcasted_iota(jnp.int32, (lp, lp), 1)
    causal = rowi >= coli
    tril = causal.astype(F32)
    triu = (rowi <= coli).astype(F32)
    valid_c = lax.broadcasted_iota(jnp.int32, (lp, 1), 0) < t_valid
    valid_r = lax.broadcasted_iota(jnp.int32, (1, lp), 1) < t_valid
    pre_c = _pad_rows(gc_ref[...], lp) + brow_ref[...]
    li_c = jnp.where(valid_c, pre_c, NEG)
    lf_c = jnp.where(valid_c, -_softplus(-pre_c), 0.0)
    b_c = jnp.dot(tril, lf_c, precision=HIGHEST, preferred_element_type=F32)
    pre_r = gt_ref[...] + bcol_ref[...]
    li_r = jnp.where(valid_r, pre_r, NEG)
    lf_r = jnp.where(valid_r, -_softplus(-pre_r), 0.0)
    b_r = jnp.dot(lf_r, triu, precision=HIGHEST, preferred_element_type=F32)

    ng = ng_ref[...]
    for h in range(ML_HEADS):
        sl = slice(h * ML_DK, (h + 1) * ML_DK)
        bc = b_c[:, ML_HEADS + h:ML_HEADS + h + 1]
        br = b_r[ML_HEADS + h:ML_HEADS + h + 1, :]
        lir = li_r[h:h + 1, :]
        lic = li_c[:, h:h + 1]
        m_prev = m_s[h:h + 1, 0:1]
        qh = q[:, sl]
        kh = k_all[:, sl]
        vh = v[:, sl].astype(BF16)
        qb = qh.astype(BF16)
        log_d = jnp.where(causal, bc - br + lir, NEG)
        log_inter = bc + m_prev
        m_t = jnp.maximum(log_inter, jnp.max(log_d, axis=-1, keepdims=True))
        s = _dot_nt(qb, kh.astype(BF16)) * jnp.exp(log_d - m_t)
        inter = jnp.exp(log_inter - m_t)
        C_h = C_s[h]
        num = _dot(s.astype(BF16), vh) + _dot(qb, C_h.astype(BF16)) * inter
        den = (jnp.sum(s, axis=-1, keepdims=True)
               + jnp.sum(qh * n_s[h:h + 1, :], axis=-1, keepdims=True) * inter)
        hh = num / jnp.maximum(jnp.abs(den), jnp.exp(-m_t))
        b_last = bc[lp - 1:lp, :]
        log_w_r = b_last - br + lir
        m_new = jnp.maximum(b_last + m_prev, jnp.max(log_w_r, axis=-1, keepdims=True))
        w_c = jnp.exp(b_last - bc + lic - m_new)
        dec = jnp.exp(b_last + m_prev - m_new)
        kw = kh * w_c
        C_s[h] = dec * C_h + _dot_tn(kw.astype(BF16), vh)
        n_s[h:h + 1, :] = dec * n_s[h:h + 1, :] + jnp.sum(kw, axis=0, keepdims=True)
        m_s[h:h + 1, :] = jnp.broadcast_to(m_new, (1, 128))
        mu = jnp.mean(hh, axis=-1, keepdims=True)
        xc = hh - mu
        var = jnp.mean(xc * xc, axis=-1, keepdims=True)
        hn = xc * lax.rsqrt(var + NORM_EPS) * ng[:, sl]
        oa_ref[:, sl] = _sigmoid(om[:, sl]) * hn[:rows]

    @pl.when(c == nc - 1)
    def _():
        C_ref[...] = C_s[...]
        n_ref[...] = n_s[...]
        m_ref[...] = m_s[...]


def mlstm_branch(z, gt, B, T, lc, conv_buf8, conv_w, bias_col, bias_row, norm_g, C0, n0, m0):
    nc = T // lc
    lp = gt.shape[-1] // nc
    lc8 = lc // 8
    kern = functools.partial(_mlstm_kernel, t_valid=lc, lp=lp, nc=nc)
    st = lambda b, c: (b, 0, 0)
    out = pl.pallas_call(
        kern,
        grid=(B, nc),
        in_specs=[
            pl.BlockSpec((lc, 2 * MIX), lambda b, c: (b * nc + c, 0)),
            pl.BlockSpec((8, 2 * MIX), lambda b, c: (jnp.maximum((b * nc + c) * lc8 - 1, 0), 0)),
            pl.BlockSpec((None, 8, 2 * MIX), lambda b, c: (b, 0, 0)),
            pl.BlockSpec((lc, MIX), lambda b, c: (b * nc + c, 2)),
            pl.BlockSpec((lc, MIX), lambda b, c: (b * nc + c, 3)),
            pl.BlockSpec((lc, 128), lambda b, c: (b * nc + c, Z_IF // 128)),
            pl.BlockSpec((None, 8, lp), lambda b, c: (b, 0, c)),
            pl.BlockSpec((ML_CONV, 2 * MIX), lambda b, c: (0, 0)),
            pl.BlockSpec((8, 1), lambda b, c: (0, 0)),
            pl.BlockSpec((1, 128), lambda b, c: (0, 0)),
            pl.BlockSpec((1, MIX), lambda b, c: (0, 0)),
            pl.BlockSpec((None, ML_HEADS, ML_DK, ML_DK), lambda b, c: (b, 0, 0, 0)),
            pl.BlockSpec((None, ML_HEADS, ML_DK), st),
            pl.BlockSpec((None, ML_HEADS, 128), st),
        ],
        out_specs=[
            pl.BlockSpec((lc, MIX), lambda b, c: (b * nc + c, 0)),
            pl.BlockSpec((None, ML_HEADS, ML_DK, ML_DK), lambda b, c: (b, 0, 0, 0)),
            pl.BlockSpec((None, ML_HEADS, ML_DK), st),
            pl.BlockSpec((None, ML_HEADS, 128), st),
        ],
        out_shape=[
            jax.ShapeDtypeStruct((B * T, MIX), F32),
            jax.ShapeDtypeStruct((B, ML_HEADS, ML_DK, ML_DK), F32),
            jax.ShapeDtypeStruct((B, ML_HEADS, ML_DK), F32),
            jax.ShapeDtypeStruct((B, ML_HEADS, 128), F32),
        ],
        scratch_shapes=[pltpu.VMEM((ML_HEADS, ML_DK, ML_DK), F32),
                        pltpu.VMEM((ML_HEADS, ML_DK), F32),
                        pltpu.VMEM((ML_HEADS, 128), F32)],
        compiler_params=_cparams("parallel", "arbitrary"),
        name="mlstm",
    )(z, z, conv_buf8, z, z, z, gt, conv_w, bias_col, bias_row, norm_g, C0, n0, m0)
    return out


def _rope_kernel(q_ref, k_ref, v_ref, cos_ref, sin_ref, ko_ref, vo_ref, *mxu_refs, transposed):
    cos = cos_ref[...]
    sin = sin_ref[...]
    lane = lax.broadcasted_iota(jnp.int32, (1, MIX), 1)
    lo = (lane % DA_HD) < (DA_HD // 2)

    def rot(x):
        partner = jnp.where(lo, pltpu.roll(x, MIX - DA_HD // 2, 1), pltpu.roll(x, DA_HD // 2, 1))
        return x * cos + partner * sin

    qr = rot(q_ref[...]) * (DA_HD ** -0.5)
    kr = rot(k_ref[...])
    v = v_ref[...]
    tm = v.shape[0]
    for h in range(DA_HEADS):
        rows_h = pl.ds(h, tm, stride=DA_HEADS)
        ko_ref[rows_h, :] = kr[:, h * DA_VD:(h + 1) * DA_VD]
        vo_ref[rows_h, :] = v[:, h * DA_VD:(h + 1) * DA_VD]
    if transposed:
        qt_ref, kb_ref, vt_ref = mxu_refs
        kb_ref[...] = kr.astype(BF16)
        for h in range(DA_HEADS):
            qt_ref[h] = qr[:, h * DA_VD:(h + 1) * DA_VD].T.astype(BF16)
            vt_ref[h] = v[:, h * DA_VD:(h + 1) * DA_VD].T.astype(BF16)
    else:
        mxu_refs[0][...] = qr


def rope_qkv(z, cos, sin, tm, B, T, transposed):
    M = z.shape[0]
    nt = cos.shape[0] // tm
    cb = Z_DA // MIX
    row = lambda i: (i, 0)
    tab = pl.BlockSpec((tm, MIX), lambda i: (i % nt, 0))
    blk = pl.BlockSpec((tm, MIX), row)
    blk4 = pl.BlockSpec((tm * DA_HEADS, DA_VD), row)
    rows4 = jax.ShapeDtypeStruct((M * DA_HEADS, DA_VD), F32)
    if transposed:
        ntq = T // tm
        blk_t = pl.BlockSpec((None, DA_HEADS, DA_VD, tm), lambda i: (i // ntq, 0, 0, i % ntq))
        shp_t = jax.ShapeDtypeStruct((B, DA_HEADS, DA_VD, T), BF16)
        extra_specs = [blk_t, blk, blk_t]
        extra_shapes = [shp_t, jax.ShapeDtypeStruct((M, MIX), BF16), shp_t]
    else:
        extra_specs = [blk]
        extra_shapes = [jax.ShapeDtypeStruct((M, MIX), F32)]
    return pl.pallas_call(
        functools.partial(_rope_kernel, transposed=transposed),
        grid=(M // tm,),
        in_specs=[pl.BlockSpec((tm, MIX), lambda i: (i, cb)),
                  pl.BlockSpec((tm, MIX), lambda i: (i, cb + 1)),
                  pl.BlockSpec((tm, MIX), lambda i: (i, cb + 2)), tab, tab],
        out_specs=[blk4, blk4] + extra_specs,
        out_shape=[rows4, rows4] + extra_shapes,
        compiler_params=_cparams("parallel"),
        name="rope_qkv",
    )(z, z, z, cos, sin)


def _subln(o, g, scale):
    ms = jnp.mean(o * o, axis=-1, keepdims=True)
    return o * lax.rsqrt(ms + NORM_EPS) * g * scale


FLASH_LANES = 128


def _flash_kernel(lam_ref, qt_ref, k_ref, vt_ref, g_ref, o_ref, qst_s, m_s, l_s, acc_s, *, tq, tk, out_scale):
    i = pl.program_id(2)
    gw = FLASH_LANES
    ng = 2 * tq // gw
    qt = qt_ref[...]
    dl = lax.broadcasted_iota(jnp.int32, (DA_VD, 1), 0)
    zero = jnp.zeros_like(qt)
    qst_s[:, :tq] = jnp.where(dl < DA_HD, qt, zero)
    qst_s[:, tq:] = jnp.where(dl >= DA_HD, qt, zero)
    m_s[...] = jnp.full(m_s.shape, NEG, F32)
    l_s[...] = jnp.zeros(l_s.shape, F32)
    acc_s[...] = jnp.zeros(acc_s.shape, F32)

    def group(k, vt, g, mask):
        gs = slice(g * gw, (g + 1) * gw)
        s = _dot(k, qst_s[:, gs])
        if mask is not None:
            s = jnp.where(mask, s, NEG)
        m_prev = m_s[0:1, gs]
        m_new = jnp.maximum(m_prev, jnp.max(s, axis=0, keepdims=True))
        alpha = jnp.exp(m_prev - m_new)
        p = jnp.exp(s - m_new)
        l_s[0:1, gs] = alpha * l_s[0:1, gs] + jnp.sum(p, axis=0, keepdims=True)
        acc_s[:, gs] = alpha * acc_s[:, gs] + _dot(vt, p.astype(BF16))
        m_s[0:1, gs] = m_new

    def body(j, carry):
        off = pl.multiple_of(j * tk, tk)
        k = k_ref[pl.ds(off, tk), :]
        vt = vt_ref[:, pl.ds(off, tk)]
        for g in range(ng):
            group(k, vt, g, None)
        return carry

    lax.fori_loop(0, i * (tq // tk), body, 0)
    kr = lax.broadcasted_iota(jnp.int32, (tk, gw), 0)
    ql = lax.broadcasted_iota(jnp.int32, (tk, gw), 1)
    for d in range(tq // tk):
        off = pl.multiple_of(i * tq + d * tk, tk)
        k = k_ref[pl.ds(off, tk), :]
        vt = vt_ref[:, pl.ds(off, tk)]
        for g in range(ng):
            a = (g * gw) % tq
            if a + gw - 1 < d * tk:
                continue
            full = a >= d * tk + tk - 1
            group(k, vt, g, None if full else (kr + d * tk) <= (ql + a))
    inv = 1.0 / l_s[0:1, :]
    o = acc_s[:, :tq] * inv[:, :tq] - lam_ref[0] * (acc_s[:, tq:] * inv[:, tq:])
    ms = jnp.mean(o * o, axis=0, keepdims=True)
    o_ref[...] = (o * lax.rsqrt(ms + NORM_EPS) * g_ref[...] * out_scale).T


def flash_diff_attention(lam, qt, kb, vt, sub_g_col, B, T, tq, tk, out_scale):
    nq = T // tq
    kern = functools.partial(_flash_kernel, tq=tq, tk=tk, out_scale=out_scale)
    return pl.pallas_call(
        kern,
        grid=(B, DA_HEADS, nq),
        in_specs=[pl.BlockSpec(memory_space=pltpu.SMEM),
                  pl.BlockSpec((None, None, DA_VD, tq), lambda b, h, i: (b, h, 0, i)),
                  pl.BlockSpec((T, DA_VD), lambda b, h, i: (b, h)),
                  pl.BlockSpec((None, None, DA_VD, T), lambda b, h, i: (b, h, 0, 0)),
                  pl.BlockSpec((DA_VD, 1), lambda b, h, i: (0, 0))],
        out_specs=pl.BlockSpec((tq, DA_VD), lambda b, h, i: (b * nq + i, h)),
        out_shape=jax.ShapeDtypeStruct((B * T, MIX), F32),
        scratch_shapes=[pltpu.VMEM((DA_VD, 2 * tq), BF16), pltpu.VMEM((8, 2 * tq), F32),
                        pltpu.VMEM((8, 2 * tq), F32), pltpu.VMEM((DA_VD, 2 * tq), F32)],
        compiler_params=_cparams("parallel", "parallel", "arbitrary"),
        name="flash_diff_attn",
    )(lam, qt, kb, vt, sub_g_col)


PAGES_PER_STEP = 8


def _dec_attn_kernel(pt_ref, lam_ref, q_ref, *refs, n_steps, tnew, out_scale):
    P = PAGES_PER_STEP
    k_refs, v_refs = refs[:P], refs[P:2 * P]
    kn_ref, vn_ref, g_ref, o_ref, w_s, m_s, l_s, acc_s = refs[2 * P:]
    j = pl.program_id(1)
    nl = DA_HEADS * 2 * tnew
    lane = lax.broadcasted_iota(jnp.int32, (1, nl), 1)
    lane_head = lane // (2 * tnew)

    @pl.when(j == 0)
    def _():
        q = q_ref[...]
        dl = lax.broadcasted_iota(jnp.int32, (1, DA_VD), 1)
        rows = []
        for h in range(DA_HEADS):
            qh = q[:, h * DA_VD:(h + 1) * DA_VD]
            rows += [jnp.where(dl < DA_HD, qh, 0.0), jnp.where(dl >= DA_HD, qh, 0.0)]
        w_s[...] = jnp.concatenate(rows, axis=0).T.astype(BF16)
        m_s[...] = jnp.full(m_s.shape, NEG, F32)
        l_s[...] = jnp.zeros(l_s.shape, F32)
        acc_s[...] = jnp.zeros(acc_s.shape, F32)

    def attend(k_of, v_of, n_blk, mask):
        w = w_s[...]
        blocks = []
        for p in range(n_blk):
            s_p = None
            for h in range(DA_HEADS):
                sh = _dot(k_of(p, h), w)
                s_p = sh if s_p is None else jnp.where(lane_head == h, sh, s_p)
            blocks.append(s_p)
        s = blocks[0] if n_blk == 1 else jnp.concatenate(blocks, axis=0)
        if mask is not None:
            s = jnp.where(mask, s, NEG)
        m_prev = m_s[0:1, :]
        m_new = jnp.maximum(m_prev, jnp.max(s, axis=0, keepdims=True))
        alpha = jnp.exp(m_prev - m_new)
        pr = jnp.exp(s - m_new)
        l_s[...] = jnp.broadcast_to(alpha * l_s[0:1, :] + jnp.sum(pr, axis=0, keepdims=True), l_s.shape)
        m_s[...] = jnp.broadcast_to(m_new, m_s.shape)
        pt = pr.T.astype(BF16)
        alpha_col = jnp.broadcast_to(alpha, (nl, nl)).T
        pv = []
        for h in range(DA_HEADS):
            r0 = h * 2 * tnew
            acc_h = None
            for p in range(n_blk):
                d = _dot(pt[r0:r0 + 2 * tnew, p * PAGE_SIZE:(p + 1) * PAGE_SIZE], v_of(p, h))
                acc_h = d if acc_h is None else acc_h + d
            pv.append(acc_h)
        acc_s[...] = alpha_col * acc_s[...] + jnp.concatenate(pv, axis=0)

    @pl.when(j < n_steps)
    def _():
        rows_h = lambda h: pl.ds(h, PAGE_SIZE, stride=DA_HEADS)
        attend(lambda p, h: k_refs[p][rows_h(h), :].astype(BF16),
               lambda p, h: v_refs[p][rows_h(h), :].astype(BF16), P, None)

    @pl.when(j == n_steps)
    def _():
        ti = lax.broadcasted_iota(jnp.int32, (PAGE_SIZE, nl), 0)
        qi = lax.broadcasted_iota(jnp.int32, (PAGE_SIZE, nl), 1) % tnew
        new_h = lambda h: pl.ds(h, tnew, stride=DA_HEADS)
        attend(lambda p, h: _pad_rows(kn_ref[new_h(h), :], PAGE_SIZE).astype(BF16),
               lambda p, h: _pad_rows(vn_ref[new_h(h), :], PAGE_SIZE).astype(BF16), 1, ti <= qi)
        lam = lam_ref[0]
        g = g_ref[...]
        l_col = jnp.broadcast_to(l_s[0:1, :], (nl, nl)).T
        a = acc_s[...] / l_col
        for h in range(DA_HEADS):
            r0 = h * 2 * tnew
            o = a[r0:r0 + tnew] - lam * a[r0 + tnew:r0 + 2 * tnew]
            o_ref[:, h * DA_VD:(h + 1) * DA_VD] = _subln(o, g, out_scale)


def paged_diff_attention(page_table, lam, q, cache_k, cache_v, l, k_new, v_new, sub_g, B, tnew, out_scale):
    P = PAGES_PER_STEP
    n_pages = page_table.shape[1]
    assert n_pages % P == 0 and DA_HEADS * 2 * tnew == 128
    n_steps = n_pages // P
    kern = functools.partial(_dec_attn_kernel, n_steps=n_steps, tnew=tnew, out_scale=out_scale)

    def page(p):
        return pl.BlockSpec((None, None, PAGE_SIZE * DA_HEADS, DA_VD),
                            lambda b, j, pt: (l, pt[b, jnp.minimum(j * P + p, n_pages - 1)], 0, 0))

    cache_k = cache_k.reshape(cache_k.shape[:2] + (PAGE_SIZE * DA_HEADS, DA_VD))
    cache_v = cache_v.reshape(cache_v.shape[:2] + (PAGE_SIZE * DA_HEADS, DA_VD))
    rows = pl.BlockSpec((tnew, MIX), lambda b, j, pt: (b, 0))
    new = pl.BlockSpec((tnew * DA_HEADS, DA_VD), lambda b, j, pt: (b, 0))
    nl = DA_HEADS * 2 * tnew
    grid_spec = pltpu.PrefetchScalarGridSpec(
        num_scalar_prefetch=1,
        grid=(B, n_steps + 1),
        in_specs=[pl.BlockSpec(memory_space=pltpu.SMEM), rows] + [page(p) for p in range(P)] * 2
                 + [new, new, pl.BlockSpec((1, DA_VD), lambda b, j, pt: (0, 0))],
        out_specs=rows,
        scratch_shapes=[pltpu.VMEM((DA_VD, nl), BF16), pltpu.VMEM((8, nl), F32), pltpu.VMEM((8, nl), F32),
                        pltpu.VMEM((nl, DA_VD), F32)],
    )
    return pl.pallas_call(
        kern,
        grid_spec=grid_spec,
        out_shape=jax.ShapeDtypeStruct((B * tnew, MIX), F32),
        compiler_params=_cparams("parallel", "arbitrary"),
        name="paged_diff_attn",
    )(page_table, lam, q, *([cache_k] * P), *([cache_v] * P), k_new, v_new, sub_g)


def _seg_sum(x, ones_bd):
    hi = x.astype(BF16)
    lo = (x - hi.astype(F32)).astype(BF16)
    return _dot(hi, ones_bd) + _dot(lo, ones_bd)


def _rwkv_prep_kernel(x_ref, halo_ref, shift_ref, mu_ref, w0_ref, a0_ref, kkw_ref, ka_ref, w2_ref, a2_ref,
                      g2_ref, ones_ref, r_ref, lw_ref, k_ref, v_ref, av_ref, bv_ref, g_ref, *, nt):
    i = pl.program_id(0)
    x = x_ref[...]
    tm = x.shape[0]
    prev_row = jnp.where(i % nt == 0, shift_ref[...], halo_ref[7:8, :])
    row = lax.broadcasted_iota(jnp.int32, (tm, 1), 0)
    prev = jnp.where(row == 0, prev_row, pltpu.roll(x, 1, 0))
    mixed = x + (prev - x) * mu_ref[...]
    r = mixed[:, :MIX]
    kr = mixed[:, MIX:2 * MIX]
    vr = mixed[:, 2 * MIX:3 * MIX]
    o = 3 * MIX
    wd = mixed[:, o:o + RW_LORA_PAD]
    ad = mixed[:, o + RW_LORA_PAD:o + 2 * RW_LORA_PAD]
    gd = mixed[:, o + 2 * RW_LORA_PAD:]
    wl = w0_ref[...] + _dot(jnp.tanh(wd).astype(BF16), w2_ref[...])
    w_log = -_softplus(-wl) - 0.5
    a = _sigmoid(a0_ref[...] + _dot(ad.astype(BF16), a2_ref[...]))
    kk = kr * kkw_ref[...]
    ss = _seg_sum(kk * kk, ones_ref[...])
    kk = kk * lax.rsqrt(jnp.maximum(ss, 1e-24))
    r_ref[...] = r
    lw_ref[...] = -jnp.exp(w_log)
    k_ref[...] = kr * (1.0 + (a - 1.0) * ka_ref[...])
    v_ref[...] = vr
    av_ref[...] = -kk
    bv_ref[...] = kk * a
    g_ref[...] = _dot(_sigmoid(gd).astype(BF16), g2_ref[...])


def rwkv_prep(z, shift_pad, nt, tm, mu, w0, a0, kkw, ka, w2, a2, g2, ones_bd):
    M = z.shape[0]
    cb = Z_RW // RW_W_PAD
    tm8 = tm // 8
    vec = pl.BlockSpec((1, MIX), lambda i: (0, 0))
    out = pl.BlockSpec((tm, MIX), lambda i: (i, 0))
    kern = functools.partial(_rwkv_prep_kernel, nt=nt)
    return pl.pallas_call(
        kern,
        grid=(M // tm,),
        in_specs=[pl.BlockSpec((tm, RW_W_PAD), lambda i: (i, cb)),
                  pl.BlockSpec((8, RW_W_PAD), lambda i: (jnp.maximum(i * tm8 - 1, 0), cb)),
                  pl.BlockSpec((None, 1, RW_W_PAD), lambda i: (i // nt, 0, 0)),
                  pl.BlockSpec((1, RW_W_PAD), lambda i: (0, 0)),
                  vec, vec, vec, vec,
                  pl.BlockSpec((RW_LORA_PAD, MIX), lambda i: (0, 0)),
                  pl.BlockSpec((RW_LORA_PAD, MIX), lambda i: (0, 0)),
                  pl.BlockSpec((RW_GATE, MIX), lambda i: (0, 0)),
                  pl.BlockSpec((MIX, MIX), lambda i: (0, 0))],
        out_specs=[out] * 7,
        out_shape=[jax.ShapeDtypeStruct((M, MIX), F32)] * 7,
        compiler_params=_cparams("parallel"),
        name="rwkv_prep",
    )(z, z, shift_pad, mu, w0, a0, kkw, ka, w2, a2, g2, ones_bd)


def _rwkv_chunk_kernel(r_ref, lw_ref, k_ref, v_ref, av_ref, bv_ref,
                       ah_ref, x_ref, rt_ref, arb_ref, yv_ref, bh_ref, z_ref, gc_ref):
    L = RW_CHUNK
    lw = _pad_rows(lw_ref[...], L)
    rowi = lax.broadcasted_iota(jnp.int32, (L, 1), 0)
    c = lw
    for s in (1, 2, 4, 8, 16, 32):
        c = c + jnp.where(rowi >= s, pltpu.roll(c, s, 0), 0.0)
    c_last = c[L - 1:L, :]
    e_neg = jnp.exp(-c)
    e_last = jnp.exp(c_last - c)
    av = _pad_rows(av_ref[...], L)
    bv = _pad_rows(bv_ref[...], L)
    kv = _pad_rows(k_ref[...], L)
    vv = _pad_rows(v_ref[...], L)
    at = av * jnp.exp(c - lw)
    rt = _pad_rows(r_ref[...], L) * jnp.exp(c)
    bt = bv * e_neg
    kt = kv * e_neg
    bh = bv * e_last
    kh = kv * e_last
    rt_ref[...] = rt.astype(BF16)
    bh_ref[...] = bh.astype(BF16)

    lane = lax.broadcasted_iota(jnp.int32, (1, 128), 1)
    m0 = lane < RW_HEAD
    r2 = lax.broadcasted_iota(jnp.int32, (128, 128), 0)
    c2 = lax.broadcasted_iota(jnp.int32, (128, 128), 1)
    same = (r2 // RW_HEAD) == (c2 // RW_HEAD)
    mask_sl = same & ((r2 % RW_HEAD) > (c2 % RW_HEAD))
    mask_l = same & ((r2 % RW_HEAD) >= (c2 % RW_HEAD))
    eye = (r2 == c2).astype(F32)
    g_last = jnp.exp(c_last)

    def dot3(a, b):
        ah, bh_ = a.astype(BF16), b.astype(BF16)
        al = (a - ah.astype(F32)).astype(BF16)
        bl = (b - bh_.astype(F32)).astype(BF16)
        return _dot(ah, bh_) + (_dot(ah, bl) + _dot(al, bh_))

    def split(x):
        return jnp.concatenate([jnp.where(m0, x, 0.0), jnp.where(m0, 0.0, x)], axis=0)

    def fold(x):
        return x[:L] + x[L:]

    pairs = range(RW_PAIRS)
    sls = [slice(p * 128, (p + 1) * 128) for p in pairs]
    a_st, v_st, n_bd, a_ak = [], [], [], []
    for p in pairs:
        sl = sls[p]
        a_st.append(split(at[:, sl]).astype(BF16))
        v_st.append(split(vv[:, sl]).astype(BF16))
        r_st = split(rt[:, sl]).astype(BF16)
        b2 = bt[:, sl].astype(BF16)
        k2 = kt[:, sl].astype(BF16)
        lhs = jnp.concatenate([a_st[p], r_st], axis=0)
        rhs = jnp.concatenate([b2, b2, k2, k2], axis=0)
        G = _dot_nt(lhs, rhs)
        n_bd.append(jnp.where(mask_sl, G[:128, :128], 0.0))
        a_ak.append(jnp.where(mask_sl, G[:128, 128:], 0.0))
        arb_ref[:, sl] = fold(jnp.where(mask_l, G[128:, :128], 0.0)).astype(BF16)
        yv_ref[:, sl] = fold(_dot(jnp.where(mask_l, G[128:, 128:], 0.0).astype(BF16), v_st[p]))
        z_ref[:, sl] = jnp.where(same, _dot_tn(kh[:, sl].astype(BF16), vv[:, sl].astype(BF16)), 0.0)
        gc_ref[:, sl] = jnp.broadcast_to(g_last[:, sl], (128, 128)).T

    base = (r2 // 16) == (c2 // 16)
    Mx = [jnp.where(base, n_bd[p], 0.0) for p in pairs]
    D = [eye + Mx[p] for p in pairs]
    for _ in range(3):
        Mx = [dot3(Mx[p], Mx[p]) for p in pairs]
        D = [D[p] + dot3(D[p], Mx[p]) for p in pairs]
    for m in (16, 32):
        off = ((r2 // (2 * m)) == (c2 // (2 * m))) & ((r2 // m) % 2 == 1) & ((c2 // m) % 2 == 0)
        dn = [_dot(D[p].astype(BF16), jnp.where(off, n_bd[p], 0.0).astype(BF16)) for p in pairs]
        D = [D[p] + _dot(dn[p].astype(BF16), D[p].astype(BF16)) for p in pairs]

    Tb = [D[p].astype(BF16) for p in pairs]
    av_st = [_dot(a_ak[p].astype(BF16), v_st[p]).astype(BF16) for p in pairs]
    for p in pairs:
        ah_ref[:, sls[p]] = fold(_dot(Tb[p], a_st[p])).astype(BF16)
        x_ref[:, sls[p]] = fold(_dot(Tb[p], av_st[p]))


def rwkv_chunk(r, lw, k, v, av, bv, rows):
    M = r.shape[0]
    nck = M // rows
    L = RW_CHUNK
    inp = pl.BlockSpec((rows, MIX), lambda i: (i, 0))
    o64 = pl.BlockSpec((L, MIX), lambda i: (i, 0))
    o128 = pl.BlockSpec((128, MIX), lambda i: (i, 0))
    s64 = lambda dt: jax.ShapeDtypeStruct((nck * L, MIX), dt)
    s128 = jax.ShapeDtypeStruct((nck * 128, MIX), F32)
    return pl.pallas_call(
        _rwkv_chunk_kernel,
        grid=(nck,),
        in_specs=[inp] * 6,
        out_specs=[o64, o64, o64, o64, o64, o64, o128, o128],
        out_shape=[s64(BF16), s64(F32), s64(BF16), s64(BF16), s64(F32), s64(BF16), s128, s128],
        compiler_params=_cparams("parallel"),
        name="rwkv_chunk",
    )(r, lw, k, v, av, bv)


def _rwkv_seq_kernel(ah_ref, x_ref, rt_ref, arb_ref, yv_ref, bh_ref, z_ref, gc_ref, h0_ref,
                     y_ref, hout_ref, h_s, *, nb):
    L = RW_CHUNK

    @pl.when(pl.program_id(0) == 0)
    def _():
        h_s[...] = h0_ref[...]

    lane = lax.broadcasted_iota(jnp.int32, (1, 128), 1)
    m0 = lane < RW_HEAD
    r2 = lax.broadcasted_iota(jnp.int32, (128, 128), 0)
    c2 = lax.broadcasted_iota(jnp.int32, (128, 128), 1)
    same = (r2 // RW_HEAD) == (c2 // RW_HEAD)
    for b in range(nb):
        for p in range(RW_PAIRS):
            sl = slice(p * 128, (p + 1) * 128)
            H = h_s[b, p]
            Hb = H.astype(BF16)
            U = _dot(ah_ref[b, :, sl], Hb) + x_ref[b, :, sl]
            u_st = jnp.concatenate([jnp.where(m0, U, 0.0), jnp.where(m0, 0.0, U)], axis=0).astype(BF16)
            y_ref[b, :, sl] = _dot(rt_ref[b, :, sl], Hb) + _dot(arb_ref[b, :, sl], u_st) + yv_ref[b, :, sl]
            upd = _dot_tn(bh_ref[b, :, sl], U.astype(BF16))
            h_s[b, p] = gc_ref[b, :, sl] * H + jnp.where(same, upd, 0.0) + z_ref[b, :, sl]

    hout_ref[...] = h_s[...]


def rwkv_seq(ah, x, rt, arb, yv, bh, zb, gc, h0, B, nc):
    L = RW_CHUNK
    t3 = lambda a, n: a.reshape(B, nc * n, MIX)
    b64 = pl.BlockSpec((B, L, MIX), lambda c: (0, c, 0))
    b128 = pl.BlockSpec((B, 128, MIX), lambda c: (0, c, 0))
    hs = pl.BlockSpec((B, RW_PAIRS, 128, 128), lambda c: (0, 0, 0, 0))
    kern = functools.partial(_rwkv_seq_kernel, nb=B)
    return pl.pallas_call(
        kern,
        grid=(nc,),
        in_specs=[b64, b64, b64, b64, b64, b64, b128, b128, hs],
        out_specs=[b64, hs],
        out_shape=[jax.ShapeDtypeStruct((B, nc * L, MIX), F32),
                   jax.ShapeDtypeStruct((B, RW_PAIRS, 128, 128), F32)],
        scratch_shapes=[pltpu.VMEM((B, RW_PAIRS, 128, 128), F32)],
        compiler_params=_cparams("arbitrary"),
        name="rwkv_seq",
    )(t3(ah, L), t3(x, L), t3(rt, L), t3(arb, L), t3(yv, L), t3(bh, L), t3(zb, 128), t3(gc, 128), h0)


def _rwkv_post_kernel(y_ref, r_ref, k_ref, v_ref, g_ref, gn_g_ref, gn_b_ref, rk_ref, ones_ref, o_ref):
    ones_bd = ones_ref[...]
    y = y_ref[...]
    inv = 1.0 / RW_HEAD
    mu = _seg_sum(y, ones_bd) * inv
    xc = y - mu
    var = _seg_sum(xc * xc, ones_bd) * inv
    yn = xc * lax.rsqrt(var + RW_GN_EPS) * gn_g_ref[...] + gn_b_ref[...]
    bonus = _seg_sum(r_ref[...] * k_ref[...] * rk_ref[...], ones_bd)
    o_ref[...] = (yn + bonus * v_ref[...]) * g_ref[...]


def rwkv_post(y3, r, k, v, g, gn_g, gn_b, rk, ones_bd, B, T, tm):
    nt = T // tm
    t3 = lambda a: a.reshape(B, T, MIX)
    blk = pl.BlockSpec((None, tm, MIX), lambda b, i: (b, i, 0))
    vec = pl.BlockSpec((1, MIX), lambda b, i: (0, 0))
    out = pl.pallas_call(
        _rwkv_post_kernel,
        grid=(B, nt),
        in_specs=[blk, blk, blk, blk, blk, vec, vec, vec, pl.BlockSpec((MIX, MIX), lambda b, i: (0, 0))],
        out_specs=blk,
        out_shape=jax.ShapeDtypeStruct((B, T, MIX), F32),
        compiler_params=_cparams("parallel", "parallel"),
        name="rwkv_post",
    )(y3, t3(r), t3(k), t3(v), t3(g), gn_g, gn_b, rk, ones_bd)
    return out.reshape(B * T, MIX)


def _regroup_w_in(w_in):
    o = 0

    def take(n):
        nonlocal o
        s = w_in[:, :, o:o + n]
        o += n
        return s

    qm, km, vm, om = take(MIX), take(MIX), take(MIX), take(MIX)
    ig, fg = take(ML_HEADS), take(ML_HEADS)
    qd, kd, vd = take(MIX), take(MIX), take(MIX)
    rw = take(RW_W)
    gates = take(3 * D_MODEL)
    zpad = lambda n: jnp.zeros(w_in.shape[:2] + (n,), w_in.dtype)
    rw_main = rw[:, :, :3 * MIX]
    wd = rw[:, :, 3 * MIX:3 * MIX + RW_LORA]
    ad = rw[:, :, 3 * MIX + RW_LORA:3 * MIX + 2 * RW_LORA]
    gd = rw[:, :, 3 * MIX + 2 * RW_LORA:]
    lp = RW_LORA_PAD - RW_LORA
    cols = [qm, km, vm, om, qd, kd, vd, rw_main, wd, zpad(lp), ad, zpad(lp), gd, gates,
            ig, fg, zpad(Z_IF_W - 2 * ML_HEADS)]
    return jnp.concatenate(cols, axis=-1).astype(BF16)


def _pad_rw_row(v):
    lp = RW_LORA_PAD - RW_LORA
    z = jnp.zeros(v.shape[:-1] + (lp,), v.dtype)
    o = 3 * MIX
    return jnp.concatenate([v[..., :o], v[..., o:o + RW_LORA], z, v[..., o + RW_LORA:o + 2 * RW_LORA], z,
                            v[..., o + 2 * RW_LORA:]], axis=-1)


def _unpad_rw_row(v):
    o = 3 * MIX
    return jnp.concatenate([v[..., :o], v[..., o:o + RW_LORA], v[..., o + RW_LORA_PAD:o + RW_LORA_PAD + RW_LORA],
                            v[..., o + 2 * RW_LORA_PAD:]], axis=-1)


def _rope_tables(pos):
    half = DA_HD // 2
    inv = 1.0 / (ROPE_THETA ** (jnp.arange(0, DA_HD, 2, dtype=F32) / DA_HD))
    ang = pos.astype(F32)[:, None] * inv[None, :]
    cos = jnp.cos(ang)
    sin = jnp.sin(ang)
    cos_g = jnp.concatenate([cos, cos], axis=-1)
    sin_g = jnp.concatenate([-sin, sin], axis=-1)
    reps = MIX // DA_HD
    return jnp.tile(cos_g, (1, reps)), jnp.tile(sin_g, (1, reps))


def _pack_rw_state(S):
    B = S.shape[0]
    St = jnp.swapaxes(S, -1, -2).reshape(B, RW_PAIRS, 2, RW_HEAD, RW_HEAD)
    z = jnp.zeros_like(St[:, :, 0])
    top = jnp.concatenate([St[:, :, 0], z], axis=-1)
    bot = jnp.concatenate([z, St[:, :, 1]], axis=-1)
    return jnp.concatenate([top, bot], axis=-2)


def _unpack_rw_state(Hbd):
    B = Hbd.shape[0]
    h0 = Hbd[:, :, :RW_HEAD, :RW_HEAD]
    h1 = Hbd[:, :, RW_HEAD:, RW_HEAD:]
    St = jnp.stack([h0, h1], axis=2).reshape(B, RW_HEADS, RW_HEAD, RW_HEAD)
    return jnp.swapaxes(St, -1, -2)


def _layer(x, l, W, P, grp):
    B, T = grp["B"], grp["T"]
    M = B * T
    tm = grp["tm"]
    z = norm_matmul(x, P["norm1_g"][l][None], W["w_in"], l, tm, grp["in_tn"])
    z3 = z.reshape(B, T, N_Z)

    lc = grp["ml_chunk"]
    lp = grp["ml_pad"]
    g8 = z[:, Z_IF:Z_IF + 8].reshape(B, T, 8)
    gt = jnp.swapaxes(g8, 1, 2)
    if lp != lc:
        gt = jnp.pad(gt, ((0, 0), (0, 0), (0, lp - lc)))
    bias = jnp.concatenate([P["ml_ib"][l], P["ml_fb"][l]])
    bias_row = jnp.pad(bias, (0, 120))[None]
    oa, C, n, m = mlstm_branch(z, gt, B, T, lc, grp["conv8"][l], P["ml_conv_w"][l], bias[:, None], bias_row,
                               P["ml_norm_g"][l][None], grp["C0"][l], grp["n0"][l], grp["m0"][l])
    conv_new = z3[:, T - (ML_CONV - 1):, :2 * MIX]

    k_rot, v_rows, *q_ops = rope_qkv(z, grp["cos"], grp["sin"], grp["rope_tm"], B, T, not grp["paged"])
    lam_init = 0.8 - 0.6 * math.exp(-0.3 * l)
    lpar = P["da_lam"][l]
    lam = (jnp.exp(jnp.sum(lpar[0] * lpar[1])) - jnp.exp(jnp.sum(lpar[2] * lpar[3])) + lam_init).reshape(1)
    sub_g = P["da_subln_g"][l]
    if grp["paged"]:
        ob = paged_diff_attention(grp["page_table"], lam, q_ops[0], grp["cache_k"], grp["cache_v"], l, k_rot, v_rows,
                                  sub_g[None], B, T, 1.0 - lam_init)
    else:
        qt, kb, vt = q_ops
        ob = flash_diff_attention(lam, qt, kb, vt, sub_g[:, None], B, T, grp["tq"], grp["tk"], 1.0 - lam_init)

    rtm = grp["rw_tm"]
    r, lw, kmod, vr, av, bv, g = rwkv_prep(z, grp["shift"][l], T // rtm, rtm, W["rw_mu"][l], P["rw_w0"][l][None],
                                           P["rw_a0"][l][None], P["rw_kk"][l][None], P["rw_ka"][l][None],
                                           W["rw_w2"][l], W["rw_a2"][l], W["rw_g2"][l], W["ones_bd"])
    rows = grp["rw_rows"]
    nc = T // rows
    pre = rwkv_chunk(r, lw, kmod, vr, av, bv, rows)
    y3, Hbd = rwkv_seq(*pre, grp["H0"][l], B, nc)
    oc = rwkv_post(y3, r, kmod, vr, g, P["rw_gn_g"][l][None], P["rw_gn_b"][l][None],
                   P["rw_rk"][l].reshape(1, MIX), W["ones_bd"], B, T, grp["post_tm"])
    shift_new = _unpad_rw_row(z3[:, T - 1, Z_RW:Z_RW + RW_W_PAD])
    S = _unpack_rw_state(Hbd)

    merged = merge_branches(oa, ob, oc, z, W["w_branch"], l, tm, 512)
    x = matmul_residual(merged, W["w_out"], l, x, grp["out_tm"], grp["out_tn"])
    act = ffn_in(x, P["norm2_g"][l][None], W["w_ffn_in"], l, tm, 512)
    x = matmul_residual(act, W["w_ffn_out"], l, x, grp["ffn_out_tm"], 512)
    outs = (k_rot.reshape(B, T, DA_HEADS, DA_VD), v_rows.reshape(B, T, DA_HEADS, DA_VD), conv_new,
            C, n, m[:, :, 0], shift_new, S)
    return x, outs


def _prep_weights(w_in, w_branch, w_out, w_ffn_in, w_ffn_out, rw_mu, rw_w2, rw_a2, rw_g2):
    lpad = RW_LORA_PAD - RW_LORA
    head = jnp.arange(MIX) // RW_HEAD
    return dict(
        w_in=_regroup_w_in(w_in),
        w_branch=w_branch.astype(BF16),
        w_out=w_out.astype(BF16),
        w_ffn_in=w_ffn_in.astype(BF16),
        w_ffn_out=w_ffn_out.astype(BF16),
        rw_mu=_pad_rw_row(rw_mu)[:, None],
        rw_w2=jnp.pad(rw_w2, ((0, 0), (0, lpad), (0, 0))).astype(BF16),
        rw_a2=jnp.pad(rw_a2, ((0, 0), (0, lpad), (0, 0))).astype(BF16),
        rw_g2=rw_g2.astype(BF16),
        ones_bd=(head[:, None] == head[None, :]).astype(BF16),
    )


def _prompt_group(bp, T, depth):
    zeros = lambda *s: jnp.zeros(s, F32)
    cos, sin = _rope_tables(jnp.arange(T))
    M = bp * T
    return dict(
        B=bp, T=T, tm=min(1024, M), ffn_out_tm=min(512, M), ml_chunk=min(256, T), ml_pad=min(256, T),
        rope_tm=min(512, T), tq=min(512, T), tk=min(512, T), paged=False, rw_tm=min(256, T), rw_rows=RW_CHUNK,
        in_tn=1024, out_tm=min(512, M), out_tn=D_MODEL,
        post_tm=min(512, T), norm_tm=min(512, M), cos=cos, sin=sin,
        conv8=zeros(depth, bp, 8, 2 * MIX), C0=zeros(depth, bp, ML_HEADS, ML_DK, ML_DK),
        n0=zeros(depth, bp, ML_HEADS, ML_DK), m0=zeros(depth, bp, ML_HEADS, 128),
        shift=zeros(depth, bp, 1, RW_W_PAD), H0=zeros(depth, bp, RW_PAIRS, 128, 128),
    )


def _sample_group(bs, ts, depth, cache_k, cache_v, page_table, state_ml_conv, state_ml_C, state_ml_n, state_ml_m,
                  state_rw_shift, state_rw_S):
    past = page_table.shape[1] * PAGE_SIZE
    cos, sin = _rope_tables(jnp.tile(past + jnp.arange(ts), bs))
    M = bs * ts
    return dict(
        B=bs, T=ts, tm=M, ffn_out_tm=M, ml_chunk=ts, ml_pad=RW_CHUNK, rope_tm=M, paged=True,
        in_tn=1024, out_tm=M, out_tn=D_MODEL,
        rw_tm=ts, rw_rows=ts, post_tm=ts, norm_tm=M, cos=cos, sin=sin,
        page_table=page_table, cache_k=cache_k, cache_v=cache_v,
        conv8=jnp.pad(state_ml_conv, ((0, 0), (0, 0), (8 - (ML_CONV - 1), 0), (0, 0))),
        C0=state_ml_C, n0=state_ml_n, m0=jnp.broadcast_to(state_ml_m[..., None], state_ml_m.shape + (128,)),
        shift=_pad_rw_row(state_rw_shift)[:, :, None],
        H0=jnp.stack([_pack_rw_state(state_rw_S[l]) for l in range(depth)]),
    )


def kernel(x_prompt, x_sample, cache_k, cache_v, page_table, state_ml_conv, state_ml_C, state_ml_n, state_ml_m,
           state_rw_shift, state_rw_S, norm1_g, w_in, ml_conv_w, ml_ib, ml_fb, ml_norm_g, da_lam, da_subln_g,
           rw_mu, rw_w0, rw_w2, rw_a0, rw_a2, rw_g2, rw_kk, rw_ka, rw_rk, rw_gn_g, rw_gn_b, w_branch, w_out,
           norm2_g, w_ffn_in, w_ffn_out, norm_f_g):
    P = dict(norm1_g=norm1_g, ml_conv_w=ml_conv_w, ml_ib=ml_ib, ml_fb=ml_fb, ml_norm_g=ml_norm_g, da_lam=da_lam,
             da_subln_g=da_subln_g, rw_w0=rw_w0, rw_a0=rw_a0, rw_kk=rw_kk, rw_ka=rw_ka, rw_rk=rw_rk,
             rw_gn_g=rw_gn_g, rw_gn_b=rw_gn_b, norm2_g=norm2_g)
    depth = w_in.shape[0]
    W = _prep_weights(w_in, w_branch, w_out, w_ffn_in, w_ffn_out, rw_mu, rw_w2, rw_a2, rw_g2)
    bp, T, _ = x_prompt.shape
    bs, ts, _ = x_sample.shape
    grp_p = _prompt_group(bp, T, depth)
    grp_s = _sample_group(bs, ts, depth, cache_k, cache_v, page_table, state_ml_conv, state_ml_C, state_ml_n,
                          state_ml_m, state_rw_shift, state_rw_S)

    results = []
    for x0, grp in ((x_prompt, grp_p), (x_sample, grp_s)):
        x = x0.reshape(grp["B"] * grp["T"], D_MODEL)
        outs = []
        for l in range(depth):
            x, o = _layer(x, l, W, P, grp)
            outs.append(o)
        y = final_norm(x, norm_f_g[None], grp["norm_tm"]).reshape(x0.shape)
        results.append((y, [jnp.stack([o[i] for o in outs]) for i in range(8)]))

    (y_p, sp), (y_s, ss) = results
    out = [y_p, y_s, sp[0], sp[1], ss[0], ss[1]]
    for i in range(2, 8):
        out += [sp[i], ss[i]]
    return tuple(out)
```

```python
import functools
import math

import jax
import jax.numpy as jnp
from jax import lax
from jax.experimental import pallas as pl
from jax.experimental.pallas import tpu as pltpu

F32 = jnp.float32
BF16 = jnp.bfloat16
HIGHEST = lax.Precision.HIGHEST

D_MODEL = 2048
DEPTH = 4
PAGE_SIZE = 128
MIX = D_MODEL // 2
ML_HEADS = 4
ML_DK = MIX // ML_HEADS
ML_CONV = 4
DA_HD = 64
DA_VD = 2 * DA_HD
DA_HEADS = MIX // DA_VD
ROPE_THETA = 10000.0
RW_HEAD = 64
RW_HEADS = MIX // RW_HEAD
RW_PAIRS = RW_HEADS // 2
RW_LORA = 96
RW_LORA_PAD = 128
RW_GATE = 256
RW_W = 3 * MIX + 2 * RW_LORA + RW_GATE
RW_W_PAD = 3 * MIX + 2 * RW_LORA_PAD + RW_GATE
RW_GN_EPS = 64e-5
D_FF = 5632
NORM_EPS = 1e-6
NEG = -1e30

Z_ML = 0
Z_DA = 4 * MIX
Z_RW = Z_DA + 3 * MIX
Z_GATE = Z_RW + RW_W_PAD
Z_IF = Z_GATE + 3 * D_MODEL
Z_IF_W = 512
N_Z = Z_IF + Z_IF_W

RW_CHUNK = 64
VMEM_LIMIT = 52 * 1024 * 1024


def _cparams(*sem):
    return pltpu.CompilerParams(dimension_semantics=sem, vmem_limit_bytes=VMEM_LIMIT)


def _sigmoid(x):
    return 0.5 * jnp.tanh(0.5 * x) + 0.5


def _softplus(x):
    return jnp.maximum(x, 0.0) + jnp.log1p(jnp.exp(-jnp.abs(x)))


def _dot(a, b):
    return jnp.dot(a, b, preferred_element_type=F32)


def _dot_nt(a, b):
    return lax.dot_general(a, b, (((1,), (1,)), ((), ())), preferred_element_type=F32)


def _dot_tn(a, b, precision=None):
    return lax.dot_general(a, b, (((0,), (0,)), ((), ())), preferred_element_type=F32, precision=precision)


def _pad_rows(x, rows):
    if x.shape[0] == rows:
        return x
    return jnp.concatenate([x, jnp.zeros((rows - x.shape[0],) + x.shape[1:], x.dtype)], axis=0)


def _norm_mm_kernel(x_ref, g_ref, w_ref, o_ref, hn_ref):
    @pl.when(pl.program_id(1) == 0)
    def _():
        x = x_ref[...]
        ms = jnp.mean(x * x, axis=-1, keepdims=True)
        hn_ref[...] = (x * lax.rsqrt(ms + NORM_EPS) * g_ref[...]).astype(BF16)

    o_ref[...] = _dot(hn_ref[...], w_ref[...])


def norm_matmul(x, g, w_all, l, tm, tn):
    M, D = x.shape
    N = w_all.shape[-1]
    return pl.pallas_call(
        _norm_mm_kernel,
        grid=(M // tm, N // tn),
        in_specs=[pl.BlockSpec((tm, D), lambda i, j: (i, 0)),
                  pl.BlockSpec((1, D), lambda i, j: (0, 0)),
                  pl.BlockSpec((None, D, tn), lambda i, j: (l, 0, j))],
        out_specs=pl.BlockSpec((tm, tn), lambda i, j: (i, j)),
        out_shape=jax.ShapeDtypeStruct((M, N), F32),
        scratch_shapes=[pltpu.VMEM((tm, D), BF16)],
        compiler_params=_cparams("parallel", "arbitrary"),
        name="norm_matmul",
    )(x, g, w_all)


def _ffn_in_kernel(x_ref, g_ref, wg_ref, wu_ref, o_ref, hn_ref):
    @pl.when(pl.program_id(1) == 0)
    def _():
        x = x_ref[...]
        ms = jnp.mean(x * x, axis=-1, keepdims=True)
        hn_ref[...] = (x * lax.rsqrt(ms + NORM_EPS) * g_ref[...]).astype(BF16)

    h = hn_ref[...]
    gf = _dot(h, wg_ref[...])
    uf = _dot(h, wu_ref[...])
    o_ref[...] = (gf * _sigmoid(gf) * uf).astype(BF16)


def ffn_in(x, g, w_all, l, tm, tn):
    M, D = x.shape
    nj = D_FF // tn
    return pl.pallas_call(
        _ffn_in_kernel,
        grid=(M // tm, nj),
        in_specs=[pl.BlockSpec((tm, D), lambda i, j: (i, 0)),
                  pl.BlockSpec((1, D), lambda i, j: (0, 0)),
                  pl.BlockSpec((None, D, tn), lambda i, j: (l, 0, j)),
                  pl.BlockSpec((None, D, tn), lambda i, j: (l, 0, j + nj))],
        out_specs=pl.BlockSpec((tm, tn), lambda i, j: (i, j)),
        out_shape=jax.ShapeDtypeStruct((M, D_FF), BF16),
        scratch_shapes=[pltpu.VMEM((tm, D), BF16)],
        compiler_params=_cparams("parallel", "arbitrary"),
        name="ffn_in",
    )(x, g, w_all, w_all)


def _mm_res_kernel(a_ref, w_ref, r_ref, o_ref):
    o_ref[...] = r_ref[...] + _dot(a_ref[...], w_ref[...])


def matmul_residual(a, w_all, l, res, tm, tn):
    M, K = a.shape
    N = w_all.shape[-1]
    return pl.pallas_call(
        _mm_res_kernel,
        grid=(M // tm, N // tn),
        in_specs=[pl.BlockSpec((tm, K), lambda i, j: (i, 0)),
                  pl.BlockSpec((None, K, tn), lambda i, j: (l, 0, j)),
                  pl.BlockSpec((tm, tn), lambda i, j: (i, j))],
        out_specs=pl.BlockSpec((tm, tn), lambda i, j: (i, j)),
        out_shape=jax.ShapeDtypeStruct((M, N), F32),
        compiler_params=_cparams("parallel", "parallel"),
        name="matmul_residual",
    )(a, w_all, res)


def _norm_kernel(x_ref, g_ref, o_ref):
    x = x_ref[...]
    ms = jnp.mean(x * x, axis=-1, keepdims=True)
    o_ref[...] = x * lax.rsqrt(ms + NORM_EPS) * g_ref[...]


def final_norm(x, g, tm):
    M, D = x.shape
    return pl.pallas_call(
        _norm_kernel,
        grid=(M // tm,),
        in_specs=[pl.BlockSpec((tm, D), lambda i: (i, 0)), pl.BlockSpec((1, D), lambda i: (0, 0))],
        out_specs=pl.BlockSpec((tm, D), lambda i: (i, 0)),
        out_shape=jax.ShapeDtypeStruct((M, D), F32),
        compiler_params=_cparams("parallel"),
        name="final_norm",
    )(x, g)


def _merge_kernel(a_ref, b_ref, c_ref, g0_ref, g1_ref, g2_ref, w_ref, o_ref):
    acc = _sigmoid(g0_ref[...]) * _dot(a_ref[...].astype(BF16), w_ref[0])
    acc = acc + _sigmoid(g1_ref[...]) * _dot(b_ref[...].astype(BF16), w_ref[1])
    acc = acc + _sigmoid(g2_ref[...]) * _dot(c_ref[...].astype(BF16), w_ref[2])
    o_ref[...] = acc.astype(BF16)


def merge_branches(oa, ob, oc, z, wb_all, l, tm, tn):
    M = oa.shape[0]
    nj = D_MODEL // tn
    gb = Z_GATE // tn
    br = pl.BlockSpec((tm, MIX), lambda i, j: (i, 0))

    def gate(k):
        return pl.BlockSpec((tm, tn), lambda i, j: (i, gb + k * nj + j))

    return pl.pallas_call(
        _merge_kernel,
        grid=(M // tm, nj),
        in_specs=[br, br, br, gate(0), gate(1), gate(2),
                  pl.BlockSpec((None, 3, MIX, tn), lambda i, j: (l, 0, 0, j))],
        out_specs=pl.BlockSpec((tm, tn), lambda i, j: (i, j)),
        out_shape=jax.ShapeDtypeStruct((M, D_MODEL), BF16),
        compiler_params=_cparams("parallel", "parallel"),
        name="merge_branches",
    )(oa, ob, oc, z, z, z, wb_all)


def _mlstm_kernel(qk_ref, halo_ref, cbuf_ref, v_ref, om_ref, gc_ref, gt_ref, cw_ref, bcol_ref, brow_ref,
                  ng_ref, C0_ref, n0_ref, m0_ref,
                  oa_ref, C_ref, n_ref, m_ref, C_s, n_s, m_s, *, t_valid, lp, nc):
    c = pl.program_id(1)
    rows = qk_ref.shape[0]

    @pl.when(c == 0)
    def _():
        C_s[...] = C0_ref[...]
        n_s[...] = n0_ref[...]
        m_s[...] = m0_ref[...]

    u = _pad_rows(qk_ref[...], lp)
    halo = jnp.where(c == 0, cbuf_ref[...], halo_ref[...])
    cw = cw_ref[...]
    row8 = lax.broadcasted_iota(jnp.int32, (8, 1), 0)
    y = None
    for k in range(ML_CONV - 1, 0, -1):
        ur = pltpu.roll(u, k, 0)
        hr = pltpu.roll(halo, k, 0)
        first = jnp.where(row8 < k, hr, ur[:8])
        sh = jnp.concatenate([first, ur[8:]], axis=0)
        term = sh * cw[ML_CONV - 1 - k:ML_CONV - k]
        y = term if y is None else y + term
    y = y + u * cw[ML_CONV - 1:ML_CONV]
    act = y * _sigmoid(y)
    q = act[:, :MIX]
    k_all = act[:, MIX:] * (ML_DK ** -0.5)
    v = _pad_rows(v_ref[...], lp)
    om = om_ref[...]

    rowi = lax.broadcasted_iota(jnp.int32, (lp, lp), 0)
    coli = lax.broadcasted_iota(jnp.int32, (lp, lp), 1)
    causal = rowi >= coli
    tril = causal.astype(F32)
    triu = (rowi <= coli).astype(F32)
    valid_c = lax.broadcasted_iota(jnp.int32, (lp, 1), 0) < t_valid
    valid_r = lax.broadcasted_iota(jnp.int32, (1, lp), 1) < t_valid
    pre_c = _pad_rows(gc_ref[...], lp) + brow_ref[...]
    li_c = jnp.where(valid_c, pre_c, NEG)
    lf_c = jnp.where(valid_c, -_softplus(-pre_c), 0.0)
    b_c = jnp.dot(tril, lf_c, precision=HIGHEST, preferred_element_type=F32)
    pre_r = gt_ref[...] + bcol_ref[...]
    li_r = jnp.where(valid_r, pre_r, NEG)
    lf_r = jnp.where(valid_r, -_softplus(-pre_r), 0.0)
    b_r = jnp.dot(lf_r, triu, precision=HIGHEST, preferred_element_type=F32)

    ng = ng_ref[...]
    for h in range(ML_HEADS):
        sl = slice(h * ML_DK, (h + 1) * ML_DK)
        bc = b_c[:, ML_HEADS + h:ML_HEADS + h + 1]
        br = b_r[ML_HEADS + h:ML_HEADS + h + 1, :]
        lir = li_r[h:h + 1, :]
        lic = li_c[:, h:h + 1]
        m_prev = m_s[h:h + 1, 0:1]
        qh = q[:, sl]
        kh = k_all[:, sl]
        vh = v[:, sl].astype(BF16)
        qb = qh.astype(BF16)
        log_d = jnp.where(causal, bc - br + lir, NEG)
        log_inter = bc + m_prev
        m_t = jnp.maximum(log_inter, jnp.max(log_d, axis=-1, keepdims=True))
        s = _dot_nt(qb, kh.astype(BF16)) * jnp.exp(log_d - m_t)
        inter = jnp.exp(log_inter - m_t)
        C_h = C_s[h]
        num = _dot(s.astype(BF16), vh) + _dot(qb, C_h.astype(BF16)) * inter
        den = (jnp.sum(s, axis=-1, keepdims=True)
               + jnp.sum(qh * n_s[h:h + 1, :], axis=-1, keepdims=True) * inter)
        hh = num / jnp.maximum(jnp.abs(den), jnp.exp(-m_t))
        b_last = bc[lp - 1:lp, :]
        log_w_r = b_last - br + lir
        m_new = jnp.maximum(b_last + m_prev, jnp.max(log_w_r, axis=-1, keepdims=True))
        w_c = jnp.exp(b_last - bc + lic - m_new)
        dec = jnp.exp(b_last + m_prev - m_new)
        kw = kh * w_c
        C_s[h] = dec * C_h + _dot_tn(kw.astype(BF16), vh)
        n_s[h:h + 1, :] = dec * n_s[h:h + 1, :] + jnp.sum(kw, axis=0, keepdims=True)
        m_s[h:h + 1, :] = jnp.broadcast_to(m_new, (1, 128))
        mu = jnp.mean(hh, axis=-1, keepdims=True)
        xc = hh - mu
        var = jnp.mean(xc * xc, axis=-1, keepdims=True)
        hn = xc * lax.rsqrt(var + NORM_EPS) * ng[:, sl]
        oa_ref[:, sl] = (_sigmoid(om[:, sl]) * hn[:rows]).astype(oa_ref.dtype)

    @pl.when(c == nc - 1)
    def _():
        C_ref[...] = C_s[...]
        n_ref[...] = n_s[...]
        m_ref[...] = m_s[...]


def mlstm_branch(z, gt, B, T, lc, conv_buf8, conv_w, bias_col, bias_row, norm_g, C0, n0, m0, out_dtype):
    nc = T // lc
    lp = gt.shape[-1] // nc
    lc8 = lc // 8
    kern = functools.partial(_mlstm_kernel, t_valid=lc, lp=lp, nc=nc)
    st = lambda b, c: (b, 0, 0)
    out = pl.pallas_call(
        kern,
        grid=(B, nc),
        in_specs=[
            pl.BlockSpec((lc, 2 * MIX), lambda b, c: (b * nc + c, 0)),
            pl.BlockSpec((8, 2 * MIX), lambda b, c: (jnp.maximum((b * nc + c) * lc8 - 1, 0), 0)),
            pl.BlockSpec((None, 8, 2 * MIX), lambda b, c: (b, 0, 0)),
            pl.BlockSpec((lc, MIX), lambda b, c: (b * nc + c, 2)),
            pl.BlockSpec((lc, MIX), lambda b, c: (b * nc + c, 3)),
            pl.BlockSpec((lc, 128), lambda b, c: (b * nc + c, Z_IF // 128)),
            pl.BlockSpec((None, 8, lp), lambda b, c: (b, 0, c)),
            pl.BlockSpec((ML_CONV, 2 * MIX), lambda b, c: (0, 0)),
            pl.BlockSpec((8, 1), lambda b, c: (0, 0)),
            pl.BlockSpec((1, 128), lambda b, c: (0, 0)),
            pl.BlockSpec((1, MIX), lambda b, c: (0, 0)),
            pl.BlockSpec((None, ML_HEADS, ML_DK, ML_DK), lambda b, c: (b, 0, 0, 0)),
            pl.BlockSpec((None, ML_HEADS, ML_DK), st),
            pl.BlockSpec((None, ML_HEADS, 128), st),
        ],
        out_specs=[
            pl.BlockSpec((lc, MIX), lambda b, c: (b * nc + c, 0)),
            pl.BlockSpec((None, ML_HEADS, ML_DK, ML_DK), lambda b, c: (b, 0, 0, 0)),
            pl.BlockSpec((None, ML_HEADS, ML_DK), st),
            pl.BlockSpec((None, ML_HEADS, 128), st),
        ],
        out_shape=[
            jax.ShapeDtypeStruct((B * T, MIX), out_dtype),
            jax.ShapeDtypeStruct((B, ML_HEADS, ML_DK, ML_DK), F32),
            jax.ShapeDtypeStruct((B, ML_HEADS, ML_DK), F32),
            jax.ShapeDtypeStruct((B, ML_HEADS, 128), F32),
        ],
        scratch_shapes=[pltpu.VMEM((ML_HEADS, ML_DK, ML_DK), F32),
                        pltpu.VMEM((ML_HEADS, ML_DK), F32),
                        pltpu.VMEM((ML_HEADS, 128), F32)],
        compiler_params=_cparams("parallel", "arbitrary"),
        name="mlstm",
    )(z, z, conv_buf8, z, z, z, gt, conv_w, bias_col, bias_row, norm_g, C0, n0, m0)
    return out


def _rope_kernel(q_ref, k_ref, v_ref, cos_ref, sin_ref, ko_ref, vo_ref, *mxu_refs, transposed):
    cos = cos_ref[...]
    sin = sin_ref[...]
    lane = lax.broadcasted_iota(jnp.int32, (1, MIX), 1)
    lo = (lane % DA_HD) < (DA_HD // 2)

    def rot(x):
        partner = jnp.where(lo, pltpu.roll(x, MIX - DA_HD // 2, 1), pltpu.roll(x, DA_HD // 2, 1))
        return x * cos + partner * sin

    qr = rot(q_ref[...]) * (DA_HD ** -0.5)
    kr = rot(k_ref[...])
    v = v_ref[...]
    tm = v.shape[0]
    for h in range(DA_HEADS):
        rows_h = pl.ds(h, tm, stride=DA_HEADS)
        ko_ref[rows_h, :] = kr[:, h * DA_VD:(h + 1) * DA_VD]
        vo_ref[rows_h, :] = v[:, h * DA_VD:(h + 1) * DA_VD]
    if transposed:
        qt_ref, kb_ref, vt_ref = mxu_refs
        kb_ref[...] = kr.astype(BF16)
        for h in range(DA_HEADS):
            qt_ref[h] = qr[:, h * DA_VD:(h + 1) * DA_VD].T.astype(BF16)
            vt_ref[h] = v[:, h * DA_VD:(h + 1) * DA_VD].T.astype(BF16)
    else:
        mxu_refs[0][...] = qr


def rope_qkv(z, cos, sin, tm, B, T, transposed):
    M = z.shape[0]
    nt = cos.shape[0] // tm
    cb = Z_DA // MIX
    row = lambda i: (i, 0)
    tab = pl.BlockSpec((tm, MIX), lambda i: (i % nt, 0))
    blk = pl.BlockSpec((tm, MIX), row)
    blk4 = pl.BlockSpec((tm * DA_HEADS, DA_VD), row)
    rows4 = jax.ShapeDtypeStruct((M * DA_HEADS, DA_VD), F32)
    if transposed:
        ntq = T // tm
        blk_t = pl.BlockSpec((None, DA_HEADS, DA_VD, tm), lambda i: (i // ntq, 0, 0, i % ntq))
        shp_t = jax.ShapeDtypeStruct((B, DA_HEADS, DA_VD, T), BF16)
        extra_specs = [blk_t, blk, blk_t]
        extra_shapes = [shp_t, jax.ShapeDtypeStruct((M, MIX), BF16), shp_t]
    else:
        extra_specs = [blk]
        extra_shapes = [jax.ShapeDtypeStruct((M, MIX), F32)]
    return pl.pallas_call(
        functools.partial(_rope_kernel, transposed=transposed),
        grid=(M // tm,),
        in_specs=[pl.BlockSpec((tm, MIX), lambda i: (i, cb)),
                  pl.BlockSpec((tm, MIX), lambda i: (i, cb + 1)),
                  pl.BlockSpec((tm, MIX), lambda i: (i, cb + 2)), tab, tab],
        out_specs=[blk4, blk4] + extra_specs,
        out_shape=[rows4, rows4] + extra_shapes,
        compiler_params=_cparams("parallel"),
        name="rope_qkv",
    )(z, z, z, cos, sin)


def _subln(o, g, scale):
    ms = jnp.mean(o * o, axis=-1, keepdims=True)
    return o * lax.rsqrt(ms + NORM_EPS) * g * scale


FLASH_LANES = 128


def _flash_kernel(lam_ref, qt_ref, k_ref, vt_ref, g_ref, o_ref, qst_s, m_s, l_s, acc_s, *, tq, tk, out_scale):
    i = pl.program_id(2)
    gw = FLASH_LANES
    ng = 2 * tq // gw
    qt = qt_ref[...]
    dl = lax.broadcasted_iota(jnp.int32, (DA_VD, 1), 0)
    zero = jnp.zeros_like(qt)
    qst_s[:, :tq] = jnp.where(dl < DA_HD, qt, zero)
    qst_s[:, tq:] = jnp.where(dl >= DA_HD, qt, zero)
    m_s[...] = jnp.full(m_s.shape, NEG, F32)
    l_s[...] = jnp.zeros(l_s.shape, F32)
    acc_s[...] = jnp.zeros(acc_s.shape, F32)

    def group(k, vt, g, mask):
        gs = slice(g * gw, (g + 1) * gw)
        s = _dot(k, qst_s[:, gs])
        if mask is not None:
            s = jnp.where(mask, s, NEG)
        m_prev = m_s[0:1, gs]
        m_new = jnp.maximum(m_prev, jnp.max(s, axis=0, keepdims=True))
        alpha = jnp.exp(m_prev - m_new)
        p = jnp.exp(s - m_new)
        l_s[0:1, gs] = alpha * l_s[0:1, gs] + jnp.sum(p, axis=0, keepdims=True)
        acc_s[:, gs] = alpha * acc_s[:, gs] + _dot(vt, p.astype(BF16))
        m_s[0:1, gs] = m_new

    def body(j, carry):
        off = pl.multiple_of(j * tk, tk)
        k = k_ref[pl.ds(off, tk), :]
        vt = vt_ref[:, pl.ds(off, tk)]
        for g in range(ng):
            group(k, vt, g, None)
        return carry

    lax.fori_loop(0, i * (tq // tk), body, 0)
    kr = lax.broadcasted_iota(jnp.int32, (tk, gw), 0)
    ql = lax.broadcasted_iota(jnp.int32, (tk, gw), 1)
    for d in range(tq // tk):
        off = pl.multiple_of(i * tq + d * tk, tk)
        k = k_ref[pl.ds(off, tk), :]
        vt = vt_ref[:, pl.ds(off, tk)]
        for g in range(ng):
            a = (g * gw) % tq
            if a + gw - 1 < d * tk:
                continue
            full = a >= d * tk + tk - 1
            group(k, vt, g, None if full else (kr + d * tk) <= (ql + a))
    inv = 1.0 / l_s[0:1, :]
    o = acc_s[:, :tq] * inv[:, :tq] - lam_ref[0] * (acc_s[:, tq:] * inv[:, tq:])
    ms = jnp.mean(o * o, axis=0, keepdims=True)
    o_ref[...] = (o * lax.rsqrt(ms + NORM_EPS) * g_ref[...] * out_scale).T.astype(o_ref.dtype)


def flash_diff_attention(lam, qt, kb, vt, sub_g_col, B, T, tq, tk, out_scale):
    nq = T // tq
    kern = functools.partial(_flash_kernel, tq=tq, tk=tk, out_scale=out_scale)
    return pl.pallas_call(
        kern,
        grid=(B, DA_HEADS, nq),
        in_specs=[pl.BlockSpec(memory_space=pltpu.SMEM),
                  pl.BlockSpec((None, None, DA_VD, tq), lambda b, h, i: (b, h, 0, i)),
                  pl.BlockSpec((T, DA_VD), lambda b, h, i: (b, h)),
                  pl.BlockSpec((None, None, DA_VD, T), lambda b, h, i: (b, h, 0, 0)),
                  pl.BlockSpec((DA_VD, 1), lambda b, h, i: (0, 0))],
        out_specs=pl.BlockSpec((tq, DA_VD), lambda b, h, i: (b * nq + i, h)),
        out_shape=jax.ShapeDtypeStruct((B * T, MIX), BF16),
        scratch_shapes=[pltpu.VMEM((DA_VD, 2 * tq), BF16), pltpu.VMEM((8, 2 * tq), F32),
                        pltpu.VMEM((8, 2 * tq), F32), pltpu.VMEM((DA_VD, 2 * tq), F32)],
        compiler_params=_cparams("parallel", "parallel", "arbitrary"),
        name="flash_diff_attn",
    )(lam, qt, kb, vt, sub_g_col)


PAGES_PER_STEP = 8


def _dec_attn_kernel(pt_ref, lam_ref, q_ref, *refs, n_steps, tnew, out_scale):
    P = PAGES_PER_STEP
    k_refs, v_refs = refs[:P], refs[P:2 * P]
    kn_ref, vn_ref, g_ref, o_ref, w_s, m_s, l_s, acc_s = refs[2 * P:]
    j = pl.program_id(1)
    nl = DA_HEADS * 2 * tnew
    lane = lax.broadcasted_iota(jnp.int32, (1, nl), 1)
    lane_head = lane // (2 * tnew)

    @pl.when(j == 0)
    def _():
        q = q_ref[...]
        dl = lax.broadcasted_iota(jnp.int32, (1, DA_VD), 1)
        rows = []
        for h in range(DA_HEADS):
            qh = q[:, h * DA_VD:(h + 1) * DA_VD]
            rows += [jnp.where(dl < DA_HD, qh, 0.0), jnp.where(dl >= DA_HD, qh, 0.0)]
        w_s[...] = jnp.concatenate(rows, axis=0).T.astype(BF16)
        m_s[...] = jnp.full(m_s.shape, NEG, F32)
        l_s[...] = jnp.zeros(l_s.shape, F32)
        acc_s[...] = jnp.zeros(acc_s.shape, F32)

    def attend(k_of, v_of, n_blk, mask):
        w = w_s[...]
        blocks = []
        for p in range(n_blk):
            s_p = None
            for h in range(DA_HEADS):
                sh = _dot(k_of(p, h), w)
                s_p = sh if s_p is None else jnp.where(lane_head == h, sh, s_p)
            blocks.append(s_p)
        s = blocks[0] if n_blk == 1 else jnp.concatenate(blocks, axis=0)
        if mask is not None:
            s = jnp.where(mask, s, NEG)
        m_prev = m_s[0:1, :]
        m_new = jnp.maximum(m_prev, jnp.max(s, axis=0, keepdims=True))
        alpha = jnp.exp(m_prev - m_new)
        pr = jnp.exp(s - m_new)
        l_s[...] = jnp.broadcast_to(alpha * l_s[0:1, :] + jnp.sum(pr, axis=0, keepdims=True), l_s.shape)
        m_s[...] = jnp.broadcast_to(m_new, m_s.shape)
        pt = pr.T.astype(BF16)
        alpha_col = jnp.broadcast_to(alpha, (nl, nl)).T
        pv = []
        for h in range(DA_HEADS):
            r0 = h * 2 * tnew
            acc_h = None
            for p in range(n_blk):
                d = _dot(pt[r0:r0 + 2 * tnew, p * PAGE_SIZE:(p + 1) * PAGE_SIZE], v_of(p, h))
                acc_h = d if acc_h is None else acc_h + d
            pv.append(acc_h)
        acc_s[...] = alpha_col * acc_s[...] + jnp.concatenate(pv, axis=0)

    @pl.when(j < n_steps)
    def _():
        rows_h = lambda h: pl.ds(h, PAGE_SIZE, stride=DA_HEADS)
        attend(lambda p, h: k_refs[p][rows_h(h), :].astype(BF16),
               lambda p, h: v_refs[p][rows_h(h), :].astype(BF16), P, None)

    @pl.when(j == n_steps)
    def _():
        ti = lax.broadcasted_iota(jnp.int32, (PAGE_SIZE, nl), 0)
        qi = lax.broadcasted_iota(jnp.int32, (PAGE_SIZE, nl), 1) % tnew
        new_h = lambda h: pl.ds(h, tnew, stride=DA_HEADS)
        attend(lambda p, h: _pad_rows(kn_ref[new_h(h), :], PAGE_SIZE).astype(BF16),
               lambda p, h: _pad_rows(vn_ref[new_h(h), :], PAGE_SIZE).astype(BF16), 1, ti <= qi)
        lam = lam_ref[0]
        g = g_ref[...]
        l_col = jnp.broadcast_to(l_s[0:1, :], (nl, nl)).T
        a = acc_s[...] / l_col
        for h in range(DA_HEADS):
            r0 = h * 2 * tnew
            o = a[r0:r0 + tnew] - lam * a[r0 + tnew:r0 + 2 * tnew]
            o_ref[:, h * DA_VD:(h + 1) * DA_VD] = _subln(o, g, out_scale)


def paged_diff_attention(page_table, lam, q, cache_k, cache_v, l, k_new, v_new, sub_g, B, tnew, out_scale):
    P = PAGES_PER_STEP
    n_pages = page_table.shape[1]
    assert n_pages % P == 0 and DA_HEADS * 2 * tnew == 128
    n_steps = n_pages // P
    kern = functools.partial(_dec_attn_kernel, n_steps=n_steps, tnew=tnew, out_scale=out_scale)

    def page(p):
        return pl.BlockSpec((None, None, PAGE_SIZE * DA_HEADS, DA_VD),
                            lambda b, j, pt: (l, pt[b, jnp.minimum(j * P + p, n_pages - 1)], 0, 0))

    cache_k = cache_k.reshape(cache_k.shape[:2] + (PAGE_SIZE * DA_HEADS, DA_VD))
    cache_v = cache_v.reshape(cache_v.shape[:2] + (PAGE_SIZE * DA_HEADS, DA_VD))
    rows = pl.BlockSpec((tnew, MIX), lambda b, j, pt: (b, 0))
    new = pl.BlockSpec((tnew * DA_HEADS, DA_VD), lambda b, j, pt: (b, 0))
    nl = DA_HEADS * 2 * tnew
    grid_spec = pltpu.PrefetchScalarGridSpec(
        num_scalar_prefetch=1,
        grid=(B, n_steps + 1),
        in_specs=[pl.BlockSpec(memory_space=pltpu.SMEM), rows] + [page(p) for p in range(P)] * 2
                 + [new, new, pl.BlockSpec((1, DA_VD), lambda b, j, pt: (0, 0))],
        out_specs=rows,
        scratch_shapes=[pltpu.VMEM((DA_VD, nl), BF16), pltpu.VMEM((8, nl), F32), pltpu.VMEM((8, nl), F32),
                        pltpu.VMEM((nl, DA_VD), F32)],
    )
    return pl.pallas_call(
        kern,
        grid_spec=grid_spec,
        out_shape=jax.ShapeDtypeStruct((B * tnew, MIX), F32),
        compiler_params=_cparams("parallel", "arbitrary"),
        name="paged_diff_attn",
    )(page_table, lam, q, *([cache_k] * P), *([cache_v] * P), k_new, v_new, sub_g)


def _seg_sum(x, ones_pair):
    hi = x.astype(BF16)
    lo = (x - hi.astype(F32)).astype(BF16)
    tiles = [slice(t * 128, (t + 1) * 128) for t in range(x.shape[1] // 128)]
    return jnp.concatenate([_dot(hi[:, t], ones_pair) + _dot(lo[:, t], ones_pair) for t in tiles], axis=1)


def _rwkv_prep_kernel(x_ref, halo_ref, shift_ref, mu_ref, w0_ref, a0_ref, kkw_ref, ka_ref, w2_ref, a2_ref,
                      g2_ref, ones_ref, r_ref, lw_ref, k_ref, v_ref, av_ref, bv_ref, g_ref, *, nt):
    i = pl.program_id(0)
    x = x_ref[...]
    tm = x.shape[0]
    prev_row = jnp.where(i % nt == 0, shift_ref[...], halo_ref[7:8, :])
    row = lax.broadcasted_iota(jnp.int32, (tm, 1), 0)
    prev = jnp.where(row == 0, prev_row, pltpu.roll(x, 1, 0))
    mixed = x + (prev - x) * mu_ref[...]
    r = mixed[:, :MIX]
    kr = mixed[:, MIX:2 * MIX]
    vr = mixed[:, 2 * MIX:3 * MIX]
    o = 3 * MIX
    wd = mixed[:, o:o + RW_LORA_PAD]
    ad = mixed[:, o + RW_LORA_PAD:o + 2 * RW_LORA_PAD]
    gd = mixed[:, o + 2 * RW_LORA_PAD:]
    wl = w0_ref[...] + _dot(jnp.tanh(wd).astype(BF16), w2_ref[...])
    w_log = -_softplus(-wl) - 0.5
    a = _sigmoid(a0_ref[...] + _dot(ad.astype(BF16), a2_ref[...]))
    kk = kr * kkw_ref[...]
    ss = _seg_sum(kk * kk, ones_ref[...])
    kk = kk * lax.rsqrt(jnp.maximum(ss, 1e-24))
    r_ref[...] = r
    lw_ref[...] = -jnp.exp(w_log)
    k_ref[...] = kr * (1.0 + (a - 1.0) * ka_ref[...])
    v_ref[...] = vr
    av_ref[...] = -kk
    bv_ref[...] = kk * a
    g_ref[...] = _dot(_sigmoid(gd).astype(BF16), g2_ref[...])


def rwkv_prep(z, shift_pad, nt, tm, mu, w0, a0, kkw, ka, w2, a2, g2, ones_bd):
    M = z.shape[0]
    cb = Z_RW // RW_W_PAD
    tm8 = tm // 8
    vec = pl.BlockSpec((1, MIX), lambda i: (0, 0))
    out = pl.BlockSpec((tm, MIX), lambda i: (i, 0))
    kern = functools.partial(_rwkv_prep_kernel, nt=nt)
    return pl.pallas_call(
        kern,
        grid=(M // tm,),
        in_specs=[pl.BlockSpec((tm, RW_W_PAD), lambda i: (i, cb)),
                  pl.BlockSpec((8, RW_W_PAD), lambda i: (jnp.maximum(i * tm8 - 1, 0), cb)),
                  pl.BlockSpec((None, 1, RW_W_PAD), lambda i: (i // nt, 0, 0)),
                  pl.BlockSpec((1, RW_W_PAD), lambda i: (0, 0)),
                  vec, vec, vec, vec,
                  pl.BlockSpec((RW_LORA_PAD, MIX), lambda i: (0, 0)),
                  pl.BlockSpec((RW_LORA_PAD, MIX), lambda i: (0, 0)),
                  pl.BlockSpec((RW_GATE, MIX), lambda i: (0, 0)),
                  pl.BlockSpec((128, 128), lambda i: (0, 0))],
        out_specs=[out] * 7,
        out_shape=[jax.ShapeDtypeStruct((M, MIX), F32)] * 7,
        compiler_params=_cparams("parallel"),
        name="rwkv_prep",
    )(z, z, shift_pad, mu, w0, a0, kkw, ka, w2, a2, g2, ones_bd)


def _rwkv_chunk_kernel(r_ref, lw_ref, k_ref, v_ref, av_ref, bv_ref,
                       ah_ref, x_ref, rt_ref, arb_ref, yv_ref, bh_ref, z_ref, gc_ref):
    L = RW_CHUNK
    lw = _pad_rows(lw_ref[...], L)
    rowi = lax.broadcasted_iota(jnp.int32, (L, 1), 0)
    c = lw
    for s in (1, 2, 4, 8, 16, 32):
        c = c + jnp.where(rowi >= s, pltpu.roll(c, s, 0), 0.0)
    c_last = c[L - 1:L, :]
    e_neg = jnp.exp(-c)
    e_last = jnp.exp(c_last - c)
    av = _pad_rows(av_ref[...], L)
    bv = _pad_rows(bv_ref[...], L)
    kv = _pad_rows(k_ref[...], L)
    vv = _pad_rows(v_ref[...], L)
    at = av * jnp.exp(c - lw)
    rt = _pad_rows(r_ref[...], L) * jnp.exp(c)
    bt = bv * e_neg
    kt = kv * e_neg
    bh = bv * e_last
    kh = kv * e_last
    rt_ref[...] = rt.astype(BF16)
    bh_ref[...] = bh.astype(BF16)

    lane = lax.broadcasted_iota(jnp.int32, (1, 128), 1)
    m0 = lane < RW_HEAD
    r2 = lax.broadcasted_iota(jnp.int32, (128, 128), 0)
    c2 = lax.broadcasted_iota(jnp.int32, (128, 128), 1)
    same = (r2 // RW_HEAD) == (c2 // RW_HEAD)
    mask_sl = same & ((r2 % RW_HEAD) > (c2 % RW_HEAD))
    mask_l = same & ((r2 % RW_HEAD) >= (c2 % RW_HEAD))
    eye = (r2 == c2).astype(F32)
    g_last = jnp.exp(c_last)

    def dot3(a, b):
        ah, bh_ = a.astype(BF16), b.astype(BF16)
        al = (a - ah.astype(F32)).astype(BF16)
        bl = (b - bh_.astype(F32)).astype(BF16)
        return _dot(ah, bh_) + (_dot(ah, bl) + _dot(al, bh_))

    def split(x):
        return jnp.concatenate([jnp.where(m0, x, 0.0), jnp.where(m0, 0.0, x)], axis=0)

    def fold(x):
        return x[:L] + x[L:]

    pairs = range(RW_PAIRS)
    sls = [slice(p * 128, (p + 1) * 128) for p in pairs]
    a_st, v_st, n_bd, a_ak = [], [], [], []
    for p in pairs:
        sl = sls[p]
        a_st.append(split(at[:, sl]).astype(BF16))
        v_st.append(split(vv[:, sl]).astype(BF16))
        r_st = split(rt[:, sl]).astype(BF16)
        b2 = bt[:, sl].astype(BF16)
        k2 = kt[:, sl].astype(BF16)
        lhs = jnp.concatenate([a_st[p], r_st], axis=0)
        rhs = jnp.concatenate([b2, b2, k2, k2], axis=0)
        G = _dot_nt(lhs, rhs)
        n_bd.append(jnp.where(mask_sl, G[:128, :128], 0.0))
        a_ak.append(jnp.where(mask_sl, G[:128, 128:], 0.0))
        arb_ref[:, sl] = fold(jnp.where(mask_l, G[128:, :128], 0.0)).astype(BF16)
        yv_ref[:, sl] = fold(_dot(jnp.where(mask_l, G[128:, 128:], 0.0).astype(BF16), v_st[p]))
        z_ref[:, sl] = jnp.where(same, _dot_tn(kh[:, sl].astype(BF16), vv[:, sl].astype(BF16)), 0.0)
        gc_ref[:, sl] = jnp.broadcast_to(g_last[:, sl], (128, 128)).T

    base = (r2 // 16) == (c2 // 16)
    Mx = [jnp.where(base, n_bd[p], 0.0) for p in pairs]
    D = [eye + Mx[p] for p in pairs]
    for _ in range(3):
        Mx = [dot3(Mx[p], Mx[p]) for p in pairs]
        D = [D[p] + dot3(D[p], Mx[p]) for p in pairs]
    for m in (16, 32):
        off = ((r2 // (2 * m)) == (c2 // (2 * m))) & ((r2 // m) % 2 == 1) & ((c2 // m) % 2 == 0)
        dn = [_dot(D[p].astype(BF16), jnp.where(off, n_bd[p], 0.0).astype(BF16)) for p in pairs]
        D = [D[p] + _dot(dn[p].astype(BF16), D[p].astype(BF16)) for p in pairs]

    Tb = [D[p].astype(BF16) for p in pairs]
    av_st = [_dot(a_ak[p].astype(BF16), v_st[p]).astype(BF16) for p in pairs]
    for p in pairs:
        ah_ref[:, sls[p]] = fold(_dot(Tb[p], a_st[p])).astype(BF16)
        x_ref[:, sls[p]] = fold(_dot(Tb[p], av_st[p]))


def rwkv_chunk(r, lw, k, v, av, bv, rows):
    M = r.shape[0]
    nck = M // rows
    L = RW_CHUNK
    inp = pl.BlockSpec((rows, MIX), lambda i: (i, 0))
    o64 = pl.BlockSpec((L, MIX), lambda i: (i, 0))
    o128 = pl.BlockSpec((128, MIX), lambda i: (i, 0))
    s64 = lambda dt: jax.ShapeDtypeStruct((nck * L, MIX), dt)
    s128 = jax.ShapeDtypeStruct((nck * 128, MIX), F32)
    return pl.pallas_call(
        _rwkv_chunk_kernel,
        grid=(nck,),
        in_specs=[inp] * 6,
        out_specs=[o64, o64, o64, o64, o64, o64, o128, o128],
        out_shape=[s64(BF16), s64(F32), s64(BF16), s64(BF16), s64(F32), s64(BF16), s128, s128],
        compiler_params=_cparams("parallel"),
        name="rwkv_chunk",
    )(r, lw, k, v, av, bv)


def _rwkv_seq_kernel(ah_ref, x_ref, rt_ref, arb_ref, yv_ref, bh_ref, z_ref, gc_ref, h0_ref,
                     y_ref, hout_ref, h_s, *, nb):
    L = RW_CHUNK

    @pl.when(pl.program_id(0) == 0)
    def _():
        h_s[...] = h0_ref[...]

    lane = lax.broadcasted_iota(jnp.int32, (1, 128), 1)
    m0 = lane < RW_HEAD
    r2 = lax.broadcasted_iota(jnp.int32, (128, 128), 0)
    c2 = lax.broadcasted_iota(jnp.int32, (128, 128), 1)
    same = (r2 // RW_HEAD) == (c2 // RW_HEAD)
    chains = [(b, p, slice(p * 128, (p + 1) * 128)) for b in range(nb) for p in range(RW_PAIRS)]
    H = [h_s[b, p] for b, p, _ in chains]
    Hb = [h.astype(BF16) for h in H]
    U = [_dot(ah_ref[b, :, sl], Hb[i]) + x_ref[b, :, sl] for i, (b, p, sl) in enumerate(chains)]
    upd = [_dot_tn(bh_ref[b, :, sl], U[i].astype(BF16)) for i, (b, p, sl) in enumerate(chains)]
    for i, (b, p, sl) in enumerate(chains):
        h_s[b, p] = gc_ref[b, :, sl] * H[i] + jnp.where(same, upd[i], 0.0) + z_ref[b, :, sl]
    for i, (b, p, sl) in enumerate(chains):
        u_st = jnp.concatenate([jnp.where(m0, U[i], 0.0), jnp.where(m0, 0.0, U[i])], axis=0).astype(BF16)
        y_ref[b, :, sl] = _dot(rt_ref[b, :, sl], Hb[i]) + _dot(arb_ref[b, :, sl], u_st) + yv_ref[b, :, sl]

    @pl.when(pl.program_id(0) == pl.num_programs(0) - 1)
    def _():
        hout_ref[...] = h_s[...]


def rwkv_seq(ah, x, rt, arb, yv, bh, zb, gc, h0, B, nc):
    L = RW_CHUNK
    t3 = lambda a, n: a.reshape(B, nc * n, MIX)
    b64 = pl.BlockSpec((B, L, MIX), lambda c: (0, c, 0))
    b128 = pl.BlockSpec((B, 128, MIX), lambda c: (0, c, 0))
    hs = pl.BlockSpec((B, RW_PAIRS, 128, 128), lambda c: (0, 0, 0, 0))
    kern = functools.partial(_rwkv_seq_kernel, nb=B)
    return pl.pallas_call(
        kern,
        grid=(nc,),
        in_specs=[b64, b64, b64, b64, b64, b64, b128, b128, hs],
        out_specs=[b64, hs],
        out_shape=[jax.ShapeDtypeStruct((B, nc * L, MIX), F32),
                   jax.ShapeDtypeStruct((B, RW_PAIRS, 128, 128), F32)],
        scratch_shapes=[pltpu.VMEM((B, RW_PAIRS, 128, 128), F32)],
        compiler_params=_cparams("arbitrary"),
        name="rwkv_seq",
    )(t3(ah, L), t3(x, L), t3(rt, L), t3(arb, L), t3(yv, L), t3(bh, L), t3(zb, 128), t3(gc, 128), h0)


def _rwkv_post_kernel(y_ref, r_ref, k_ref, v_ref, g_ref, gn_g_ref, gn_b_ref, rk_ref, ones_ref, o_ref):
    ones_bd = ones_ref[...]
    y = y_ref[...]
    inv = 1.0 / RW_HEAD
    mu = _seg_sum(y, ones_bd) * inv
    xc = y - mu
    var = _seg_sum(xc * xc, ones_bd) * inv
    yn = xc * lax.rsqrt(var + RW_GN_EPS) * gn_g_ref[...] + gn_b_ref[...]
    bonus = _seg_sum(r_ref[...] * k_ref[...] * rk_ref[...], ones_bd)
    o_ref[...] = ((yn + bonus * v_ref[...]) * g_ref[...]).astype(o_ref.dtype)


def rwkv_post(y3, r, k, v, g, gn_g, gn_b, rk, ones_bd, B, T, tm, out_dtype):
    nt = T // tm
    t3 = lambda a: a.reshape(B, T, MIX)
    blk = pl.BlockSpec((None, tm, MIX), lambda b, i: (b, i, 0))
    vec = pl.BlockSpec((1, MIX), lambda b, i: (0, 0))
    out = pl.pallas_call(
        _rwkv_post_kernel,
        grid=(B, nt),
        in_specs=[blk, blk, blk, blk, blk, vec, vec, vec, pl.BlockSpec((128, 128), lambda b, i: (0, 0))],
        out_specs=blk,
        out_shape=jax.ShapeDtypeStruct((B, T, MIX), out_dtype),
        compiler_params=_cparams("parallel", "parallel"),
        name="rwkv_post",
    )(y3, t3(r), t3(k), t3(v), t3(g), gn_g, gn_b, rk, ones_bd)
    return out.reshape(B * T, MIX)


def _regroup_w_in(w_in):
    o = 0

    def take(n):
        nonlocal o
        s = w_in[:, :, o:o + n]
        o += n
        return s

    qm, km, vm, om = take(MIX), take(MIX), take(MIX), take(MIX)
    ig, fg = take(ML_HEADS), take(ML_HEADS)
    qd, kd, vd = take(MIX), take(MIX), take(MIX)
    rw = take(RW_W)
    gates = take(3 * D_MODEL)
    zpad = lambda n: jnp.zeros(w_in.shape[:2] + (n,), w_in.dtype)
    rw_main = rw[:, :, :3 * MIX]
    wd = rw[:, :, 3 * MIX:3 * MIX + RW_LORA]
    ad = rw[:, :, 3 * MIX + RW_LORA:3 * MIX + 2 * RW_LORA]
    gd = rw[:, :, 3 * MIX + 2 * RW_LORA:]
    lp = RW_LORA_PAD - RW_LORA
    cols = [qm, km, vm, om, qd, kd, vd, rw_main, wd, zpad(lp), ad, zpad(lp), gd, gates,
            ig, fg, zpad(Z_IF_W - 2 * ML_HEADS)]
    return jnp.concatenate(cols, axis=-1).astype(BF16)


def _pad_rw_row(v):
    lp = RW_LORA_PAD - RW_LORA
    z = jnp.zeros(v.shape[:-1] + (lp,), v.dtype)
    o = 3 * MIX
    return jnp.concatenate([v[..., :o], v[..., o:o + RW_LORA], z, v[..., o + RW_LORA:o + 2 * RW_LORA], z,
                            v[..., o + 2 * RW_LORA:]], axis=-1)


def _unpad_rw_row(v):
    o = 3 * MIX
    return jnp.concatenate([v[..., :o], v[..., o:o + RW_LORA], v[..., o + RW_LORA_PAD:o + RW_LORA_PAD + RW_LORA],
                            v[..., o + 2 * RW_LORA_PAD:]], axis=-1)


def _rope_tables(pos):
    half = DA_HD // 2
    inv = 1.0 / (ROPE_THETA ** (jnp.arange(0, DA_HD, 2, dtype=F32) / DA_HD))
    ang = pos.astype(F32)[:, None] * inv[None, :]
    cos = jnp.cos(ang)
    sin = jnp.sin(ang)
    cos_g = jnp.concatenate([cos, cos], axis=-1)
    sin_g = jnp.concatenate([-sin, sin], axis=-1)
    reps = MIX // DA_HD
    return jnp.tile(cos_g, (1, reps)), jnp.tile(sin_g, (1, reps))


def _pack_rw_state(S):
    B = S.shape[0]
    St = jnp.swapaxes(S, -1, -2).reshape(B, RW_PAIRS, 2, RW_HEAD, RW_HEAD)
    z = jnp.zeros_like(St[:, :, 0])
    top = jnp.concatenate([St[:, :, 0], z], axis=-1)
    bot = jnp.concatenate([z, St[:, :, 1]], axis=-1)
    return jnp.concatenate([top, bot], axis=-2)


def _unpack_rw_state(Hbd):
    B = Hbd.shape[0]
    h0 = Hbd[:, :, :RW_HEAD, :RW_HEAD]
    h1 = Hbd[:, :, RW_HEAD:, RW_HEAD:]
    St = jnp.stack([h0, h1], axis=2).reshape(B, RW_HEADS, RW_HEAD, RW_HEAD)
    return jnp.swapaxes(St, -1, -2)


def _layer(x, l, W, P, grp):
    B, T = grp["B"], grp["T"]
    M = B * T
    tm = grp["tm"]
    z = norm_matmul(x, P["norm1_g"][l][None], W["w_in"], l, tm, grp["in_tn"])
    z3 = z.reshape(B, T, N_Z)

    lc = grp["ml_chunk"]
    lp = grp["ml_pad"]
    g8 = z[:, Z_IF:Z_IF + 8].reshape(B, T, 8)
    gt = jnp.swapaxes(g8, 1, 2)
    if lp != lc:
        gt = jnp.pad(gt, ((0, 0), (0, 0), (0, lp - lc)))
    bias = jnp.concatenate([P["ml_ib"][l], P["ml_fb"][l]])
    bias_row = jnp.pad(bias, (0, 120))[None]
    oa, C, n, m = mlstm_branch(z, gt, B, T, lc, grp["conv8"][l], P["ml_conv_w"][l], bias[:, None], bias_row,
                               P["ml_norm_g"][l][None], grp["C0"][l], grp["n0"][l], grp["m0"][l], grp["br_dtype"])
    conv_new = z3[:, T - (ML_CONV - 1):, :2 * MIX]

    k_rot, v_rows, *q_ops = rope_qkv(z, grp["cos"], grp["sin"], grp["rope_tm"], B, T, not grp["paged"])
    lam_init = 0.8 - 0.6 * math.exp(-0.3 * l)
    lpar = P["da_lam"][l]
    lam = (jnp.exp(jnp.sum(lpar[0] * lpar[1])) - jnp.exp(jnp.sum(lpar[2] * lpar[3])) + lam_init).reshape(1)
    sub_g = P["da_subln_g"][l]
    if grp["paged"]:
        ob = paged_diff_attention(grp["page_table"], lam, q_ops[0], grp["cache_k"], grp["cache_v"], l, k_rot, v_rows,
                                  sub_g[None], B, T, 1.0 - lam_init)
    else:
        qt, kb, vt = q_ops
        ob = flash_diff_attention(lam, qt, kb, vt, sub_g[:, None], B, T, grp["tq"], grp["tk"], 1.0 - lam_init)

    rtm = grp["rw_tm"]
    r, lw, kmod, vr, av, bv, g = rwkv_prep(z, grp["shift"][l], T // rtm, rtm, W["rw_mu"][l], P["rw_w0"][l][None],
                                           P["rw_a0"][l][None], P["rw_kk"][l][None], P["rw_ka"][l][None],
                                           W["rw_w2"][l], W["rw_a2"][l], W["rw_g2"][l], W["ones_bd"])
    rows = grp["rw_rows"]
    nc = T // rows
    pre = rwkv_chunk(r, lw, kmod, vr, av, bv, rows)
    y3, Hbd = rwkv_seq(*pre, grp["H0"][l], B, nc)
    oc = rwkv_post(y3, r, kmod, vr, g, P["rw_gn_g"][l][None], P["rw_gn_b"][l][None],
                   P["rw_rk"][l].reshape(1, MIX), W["ones_bd"], B, T, grp["post_tm"], grp["br_dtype"])
    shift_new = _unpad_rw_row(z3[:, T - 1, Z_RW:Z_RW + RW_W_PAD])
    S = _unpack_rw_state(Hbd)

    merged = merge_branches(oa, ob, oc, z, W["w_branch"], l, tm, 512)
    x = matmul_residual(merged, W["w_out"], l, x, grp["out_tm"], grp["out_tn"])
    act = ffn_in(x, P["norm2_g"][l][None], W["w_ffn_in"], l, tm, 512)
    x = matmul_residual(act, W["w_ffn_out"], l, x, grp["ffn_out_tm"], 512)
    outs = (k_rot.reshape(B, T, DA_HEADS, DA_VD), v_rows.reshape(B, T, DA_HEADS, DA_VD), conv_new,
            C, n, m[:, :, 0], shift_new, S)
    return x, outs


def _prep_weights(w_in, w_branch, w_out, w_ffn_in, w_ffn_out, rw_mu, rw_w2, rw_a2, rw_g2):
    lpad = RW_LORA_PAD - RW_LORA
    head = jnp.arange(2 * RW_HEAD) // RW_HEAD
    return dict(
        w_in=_regroup_w_in(w_in),
        w_branch=w_branch.astype(BF16),
        w_out=w_out.astype(BF16),
        w_ffn_in=w_ffn_in.astype(BF16),
        w_ffn_out=w_ffn_out.astype(BF16),
        rw_mu=_pad_rw_row(rw_mu)[:, None],
        rw_w2=jnp.pad(rw_w2, ((0, 0), (0, lpad), (0, 0))).astype(BF16),
        rw_a2=jnp.pad(rw_a2, ((0, 0), (0, lpad), (0, 0))).astype(BF16),
        rw_g2=rw_g2.astype(BF16),
        ones_bd=(head[:, None] == head[None, :]).astype(BF16),
    )


def _prompt_group(bp, T, depth):
    zeros = lambda *s: jnp.zeros(s, F32)
    cos, sin = _rope_tables(jnp.arange(T))
    M = bp * T
    return dict(
        B=bp, T=T, tm=min(1024, M), ffn_out_tm=min(512, M), ml_chunk=min(256, T), ml_pad=min(256, T),
        rope_tm=min(512, T), tq=min(1024, T), tk=min(512, T), paged=False, rw_tm=min(256, T), rw_rows=RW_CHUNK,
        in_tn=1024, out_tm=min(512, M), out_tn=D_MODEL, br_dtype=BF16,
        post_tm=min(512, T), norm_tm=min(512, M), cos=cos, sin=sin,
        conv8=zeros(depth, bp, 8, 2 * MIX), C0=zeros(depth, bp, ML_HEADS, ML_DK, ML_DK),
        n0=zeros(depth, bp, ML_HEADS, ML_DK), m0=zeros(depth, bp, ML_HEADS, 128),
        shift=zeros(depth, bp, 1, RW_W_PAD), H0=zeros(depth, bp, RW_PAIRS, 128, 128),
    )


def _sample_group(bs, ts, depth, cache_k, cache_v, page_table, state_ml_conv, state_ml_C, state_ml_n, state_ml_m,
                  state_rw_shift, state_rw_S):
    past = page_table.shape[1] * PAGE_SIZE
    cos, sin = _rope_tables(jnp.tile(past + jnp.arange(ts), bs))
    M = bs * ts
    return dict(
        B=bs, T=ts, tm=M, ffn_out_tm=M, ml_chunk=ts, ml_pad=RW_CHUNK, rope_tm=M, paged=True,
        in_tn=1024, out_tm=M, out_tn=D_MODEL, br_dtype=F32,
        rw_tm=ts, rw_rows=ts, post_tm=ts, norm_tm=M, cos=cos, sin=sin,
        page_table=page_table, cache_k=cache_k, cache_v=cache_v,
        conv8=jnp.pad(state_ml_conv, ((0, 0), (0, 0), (8 - (ML_CONV - 1), 0), (0, 0))),
        C0=state_ml_C, n0=state_ml_n, m0=jnp.broadcast_to(state_ml_m[..., None], state_ml_m.shape + (128,)),
        shift=_pad_rw_row(state_rw_shift)[:, :, None],
        H0=jnp.stack([_pack_rw_state(state_rw_S[l]) for l in range(depth)]),
    )


def kernel(x_prompt, x_sample, cache_k, cache_v, page_table, state_ml_conv, state_ml_C, state_ml_n, state_ml_m,
           state_rw_shift, state_rw_S, norm1_g, w_in, ml_conv_w, ml_ib, ml_fb, ml_norm_g, da_lam, da_subln_g,
           rw_mu, rw_w0, rw_w2, rw_a0, rw_a2, rw_g2, rw_kk, rw_ka, rw_rk, rw_gn_g, rw_gn_b, w_branch, w_out,
           norm2_g, w_ffn_in, w_ffn_out, norm_f_g):
    P = dict(norm1_g=norm1_g, ml_conv_w=ml_conv_w, ml_ib=ml_ib, ml_fb=ml_fb, ml_norm_g=ml_norm_g, da_lam=da_lam,
             da_subln_g=da_subln_g, rw_w0=rw_w0, rw_a0=rw_a0, rw_kk=rw_kk, rw_ka=rw_ka, rw_rk=rw_rk,
             rw_gn_g=rw_gn_g, rw_gn_b=rw_gn_b, norm2_g=norm2_g)
    depth = w_in.shape[0]
    W = _prep_weights(w_in, w_branch, w_out, w_ffn_in, w_ffn_out, rw_mu, rw_w2, rw_a2, rw_g2)
    bp, T, _ = x_prompt.shape
    bs, ts, _ = x_sample.shape
    grp_p = _prompt_group(bp, T, depth)
    grp_s = _sample_group(bs, ts, depth, cache_k, cache_v, page_table, state_ml_conv, state_ml_C, state_ml_n,
                          state_ml_m, state_rw_shift, state_rw_S)

    results = []
    for x0, grp in ((x_prompt, grp_p), (x_sample, grp_s)):
        x = x0.reshape(grp["B"] * grp["T"], D_MODEL)
        outs = []
        for l in range(depth):
            x, o = _layer(x, l, W, P, grp)
            outs.append(o)
        y = final_norm(x, norm_f_g[None], grp["norm_tm"]).reshape(x0.shape)
        results.append((y, [jnp.stack([o[i] for o in outs]) for i in range(8)]))

    (y_p, sp), (y_s, ss) = results
    out = [y_p, y_s, sp[0], sp[1], ss[0], ss[1]]
    for i in range(2, 8):
        out += [sp[i], ss[i]]
    return tuple(out)
```
